```python
import jax
import jax.numpy as jnp
from jax import lax
import numpy as np

D_MODEL = 1024
BATCH = 4
SEQ = 4096
DEPTH = 4
DEC_BATCH = 16
DEC_SEQ = 2048
PAST_LEN = 128

HEAD_DIM = 64
N_HEADS = D_MODEL // HEAD_DIM
MIX_WIDTH = N_HEADS * HEAD_DIM
Q_PER_KV = 2
A_HEADS = N_HEADS // 4
B_HEADS = (3 * N_HEADS) // 8
C_HEADS = N_HEADS - A_HEADS - B_HEADS
A_KV = A_HEADS // Q_PER_KV
B_KV = B_HEADS // Q_PER_KV
C_KV = C_HEADS // Q_PER_KV
KV_HEADS = A_KV + B_KV + C_KV
QKV_WIDTH = (N_HEADS + 2 * KV_HEADS) * HEAD_DIM
FFN_DIM = ((-(-8 * D_MODEL // 3) + 255) // 256) * 256

GRID_W = 64
NA_MAX_ROWS = 8
NA_COLS = 16
DILATED_BRANCHES = ((128, 1), (512, 4), (2048, 16))
Q_BLOCK = 128
ROPE_THETA = 10000.0
RMS_EPS = 1e-6
NEG_INF = -1e30

kernel_name = 'hybrid_parallel_head_encoder'


def rms_norm(x, gain):
    xf = x.astype(jnp.float32)
    y = xf * lax.rsqrt(jnp.mean(xf * xf, axis=-1, keepdims=True) + RMS_EPS)
    return (y * gain.astype(jnp.float32)).astype(x.dtype)


def rope_tables(pos, dim):
    inv_freq = ROPE_THETA ** (-jnp.arange(0, dim, 2, dtype=jnp.float32) / dim)
    ang = pos[:, None] * inv_freq[None, :]
    ang = jnp.concatenate([ang, ang], axis=-1)
    return jnp.cos(ang), jnp.sin(ang)


def apply_rope(x, cos, sin):
    xf = x.astype(jnp.float32)
    half = xf.shape[-1] // 2
    rot = jnp.concatenate([-xf[..., half:], xf[..., :half]], axis=-1)
    return (xf * cos[None, :, None, :] + rot * sin[None, :, None, :]).astype(x.dtype)


def apply_axial_rope(x, cos_r, sin_r, cos_c, sin_c):
    half = x.shape[-1] // 2
    return jnp.concatenate([apply_rope(x[..., :half], cos_r, sin_r),
                            apply_rope(x[..., half:], cos_c, sin_c)], axis=-1)


def dense_block_attention(q, k, v):
    b, t, hq, d = q.shape
    hkv = k.shape[2]
    g = hq // hkv
    nb = t // Q_BLOCK
    scale = d ** -0.5
    qb = jnp.moveaxis(q.reshape(b, nb, Q_BLOCK, hkv, g, d), 1, 0)

    def block(qi):
        s = jnp.einsum('bqhgd,bkhd->bhgqk', qi, k, preferred_element_type=jnp.float32) * scale
        p = jax.nn.softmax(s, axis=-1).astype(v.dtype)
        return jnp.einsum('bhgqk,bkhd->bqhgd', p, v)

    out = lax.map(block, qb)
    return jnp.moveaxis(out, 0, 1).reshape(b, t, hq, d)


def gathered_block_attention(q, k, v, idx, bias):
    b, t, hq, d = q.shape
    hkv = k.shape[2]
    g = hq // hkv
    nb = t // Q_BLOCK
    n_keys = idx.shape[1]
    hb = bias.shape[0]
    bias_heads = (hkv, g) if hb == hq else (1, 1)
    scale = d ** -0.5
    qb = jnp.moveaxis(q.reshape(b, nb, Q_BLOCK, hkv, g, d), 1, 0)
    ib = idx.reshape(nb, Q_BLOCK, n_keys)
    bb = jnp.moveaxis(bias.reshape(hb, nb, Q_BLOCK, n_keys), 1, 0)

    def block(args):
        qi, ii, bi = args
        kg = k[:, ii]
        vg = v[:, ii]
        s = jnp.einsum('bqhgd,bqkhd->bhgqk', qi, kg, preferred_element_type=jnp.float32) * scale
        s = s + bi.reshape(bias_heads + (Q_BLOCK, n_keys))[None].astype(jnp.float32)
        lse = jax.nn.logsumexp(s, axis=-1)
        p = jnp.exp(s - lse[..., None]).astype(v.dtype)
        o = jnp.einsum('bhgqk,bqkhd->bqhgd', p, vg)
        return o, jnp.moveaxis(lse, 3, 1)

    out, lse = lax.map(block, (qb, ib, bb))
    out = jnp.moveaxis(out, 0, 1).reshape(b, t, hq, d)
    lse = jnp.moveaxis(lse, 0, 1).reshape(b, t, hq)
    return out, lse


def neighbourhood_pattern(t):
    rows = t // GRID_W
    kh = min(NA_MAX_ROWS, rows)
    pos = jnp.arange(t, dtype=jnp.int32)
    r, c = pos // GRID_W, pos % GRID_W
    r0 = jnp.clip(r - kh // 2, 0, rows - kh)
    c0 = jnp.clip(c - NA_COLS // 2, 0, GRID_W - NA_COLS)
    key_r = r0[:, None, None] + jnp.arange(kh, dtype=jnp.int32)[None, :, None]
    key_c = c0[:, None, None] + jnp.arange(NA_COLS, dtype=jnp.int32)[None, None, :]
    idx = (key_r * GRID_W + key_c).reshape(t, kh * NA_COLS)
    off_r = jnp.broadcast_to(key_r - r[:, None, None] + (NA_MAX_ROWS - 1), (t, kh, NA_COLS)).reshape(t, -1)
    off_c = jnp.broadcast_to(key_c - c[:, None, None] + (NA_COLS - 1), (t, kh, NA_COLS)).reshape(t, -1)
    return idx, off_r, off_c


def dilated_pattern(t, window, dilation):
    half = window // (2 * dilation)
    pos = jnp.arange(t, dtype=jnp.int32)
    keys = pos[:, None] + jnp.arange(-half, half + 1, dtype=jnp.int32)[None, :] * dilation
    valid = (keys >= 0) & (keys < t)
    bias = jnp.where(valid, 0.0, NEG_INF).astype(jnp.float32)[None]
    return jnp.clip(keys, 0, t - 1), bias


def encoder_trunk(x, norm_mix, w_in, q_gain, k_gain, rpb, out_gain, w_out, norm_ffn, w_gate_up, w_down):
    b, t, _ = x.shape
    pos = jnp.arange(t, dtype=jnp.int32)
    half_dim = HEAD_DIM // 2
    cos_r, sin_r = rope_tables((pos // GRID_W).astype(jnp.float32), half_dim)
    cos_c, sin_c = rope_tables((pos % GRID_W).astype(jnp.float32), half_dim)
    cos_1d, sin_1d = rope_tables(pos.astype(jnp.float32), HEAD_DIM)
    na_idx, na_dr, na_dc = neighbourhood_pattern(t)
    dil = [dilated_pattern(t, w, d) for (w, d) in DILATED_BRANCHES]
    a_w = A_HEADS * HEAD_DIM
    b_w = B_HEADS * HEAD_DIM
    k_off = MIX_WIDTH
    v_off = MIX_WIDTH + KV_HEADS * HEAD_DIM
    for l in range(DEPTH):
        h = rms_norm(x, norm_mix[l])
        proj = h @ w_in[l]
        q = proj[..., :k_off].reshape(b, t, N_HEADS, HEAD_DIM)
        k = proj[..., k_off:v_off].reshape(b, t, KV_HEADS, HEAD_DIM)
        v = proj[..., v_off:].reshape(b, t, KV_HEADS, HEAD_DIM)

        q_a = apply_axial_rope(rms_norm(q[:, :, :A_HEADS], q_gain[l, 0]), cos_r, sin_r, cos_c, sin_c)
        k_a = apply_axial_rope(rms_norm(k[:, :, :A_KV], k_gain[l, 0]), cos_r, sin_r, cos_c, sin_c)
        o_a = dense_block_attention(q_a, k_a, v[:, :, :A_KV])

        q_b = rms_norm(q[:, :, A_HEADS:A_HEADS + B_HEADS], q_gain[l, 1])
        k_b = rms_norm(k[:, :, A_KV:A_KV + B_KV], k_gain[l, 1])
        bias_b = rpb[l][:, na_dr, na_dc]
        o_b, _ = gathered_block_attention(q_b, k_b, v[:, :, A_KV:A_KV + B_KV], na_idx, bias_b)

        q_c = apply_rope(rms_norm(q[:, :, A_HEADS + B_HEADS:], q_gain[l, 2]), cos_1d, sin_1d)
        k_c = apply_rope(rms_norm(k[:, :, A_KV + B_KV:], k_gain[l, 2]), cos_1d, sin_1d)
        v_c = v[:, :, A_KV + B_KV:]
        branches = [gathered_block_attention(q_c, k_c, v_c, idx, bias) for (idx, bias) in dil]
        outs = jnp.stack([o for (o, _) in branches]).astype(jnp.float32)
        lses = jnp.stack([s for (_, s) in branches])
        wts = jax.nn.softmax(lses, axis=0)
        o_c = jnp.einsum('nbth,nbthd->bthd', wts, outs).astype(x.dtype)

        y = jnp.concatenate([
            rms_norm(o_a.reshape(b, t, a_w), out_gain[l, :a_w]),
            rms_norm(o_b.reshape(b, t, b_w), out_gain[l, a_w:a_w + b_w]),
            rms_norm(o_c.reshape(b, t, MIX_WIDTH - a_w - b_w), out_gain[l, a_w + b_w:]),
        ], axis=-1)
        x = x + y @ w_out[l]

        h = rms_norm(x, norm_ffn[l])
        gate, up = jnp.split(h @ w_gate_up[l], 2, axis=-1)
        x = x + (jax.nn.silu(gate) * up) @ w_down[l]
    return x


def setup_inputs(seed: int = 0) -> dict:
    key = jax.random.key(seed)
    ks = jax.random.split(key, 12)
    nrm = jax.random.normal
    return {
        'x_prompt': nrm(ks[0], (BATCH, SEQ, D_MODEL), jnp.float32),
        'x_sample': nrm(ks[1], (DEC_BATCH, DEC_SEQ, D_MODEL), jnp.float32),
        'norm_mix': 1.0 + 0.02 * nrm(ks[2], (DEPTH, D_MODEL), jnp.float32),
        'w_in': nrm(ks[3], (DEPTH, D_MODEL, QKV_WIDTH), jnp.float32) * D_MODEL ** -0.5,
        'q_gain': 1.0 + 0.02 * nrm(ks[4], (DEPTH, 3, HEAD_DIM), jnp.float32),
        'k_gain': 1.0 + 0.02 * nrm(ks[5], (DEPTH, 3, HEAD_DIM), jnp.float32),
        'rpb': 0.1 * nrm(ks[6], (DEPTH, B_HEADS, 2 * NA_MAX_ROWS - 1, 2 * NA_COLS - 1), jnp.float32),
        'out_gain': 1.0 + 0.02 * nrm(ks[7], (DEPTH, MIX_WIDTH), jnp.float32),
        'w_out': nrm(ks[8], (DEPTH, MIX_WIDTH, D_MODEL), jnp.float32) * MIX_WIDTH ** -0.5,
        'norm_ffn': 1.0 + 0.02 * nrm(ks[9], (DEPTH, D_MODEL), jnp.float32),
        'w_gate_up': nrm(ks[10], (DEPTH, D_MODEL, 2 * FFN_DIM), jnp.float32) * D_MODEL ** -0.5,
        'w_down': nrm(ks[11], (DEPTH, FFN_DIM, D_MODEL), jnp.float32) * FFN_DIM ** -0.5,
    }


def reference(x_prompt, x_sample, norm_mix, w_in, q_gain, k_gain, rpb, out_gain, w_out, norm_ffn, w_gate_up, w_down):
    y_prompt = encoder_trunk(x_prompt, norm_mix, w_in, q_gain, k_gain, rpb, out_gain, w_out, norm_ffn, w_gate_up, w_down)
    y_sample = encoder_trunk(x_sample, norm_mix, w_in, q_gain, k_gain, rpb, out_gain, w_out, norm_ffn, w_gate_up, w_down)
    return (y_prompt, y_sample)
```

```python
import functools

import jax
import jax.numpy as jnp
from jax import lax
from jax.experimental import pallas as pl
from jax.experimental.pallas import tpu as pltpu

F32 = jnp.float32
BF16 = jnp.bfloat16

D_MODEL = 1024
HEAD_DIM = 64
N_HEADS = 16
KV_HEADS = 8
QKV_WIDTH = 2048
FFN_DIM = 2816
GRID_W = 64
NA_ROWS = 8
NA_COLS = 16
DILATIONS = (1, 4, 16)
HALF_WINDOW = 64
ROPE_THETA = 10000.0
RMS_EPS = 1e-6
NEG_INF = -1e30
SCALE = HEAD_DIM ** -0.5

LANES = 128
PAIR = 2 * LANES
A_GROUPS, B_GROUPS, C_GROUPS = 2, 3, 3
Q_BLK = {"a": 0, "b": 2, "c": 5}
KV_BLK = {"a": 8, "b": 10, "c": 13}
N_BLKS = QKV_WIDTH // LANES

VMEM_LIMIT = 56 * 1024 * 1024


def _cparams(n_axes):
    return pltpu.CompilerParams(
        dimension_semantics=("arbitrary",) * n_axes, vmem_limit_bytes=VMEM_LIMIT)


def _resident(block_shape, index_map):
    return pl.BlockSpec(block_shape, index_map, pipeline_mode=pl.Buffered(1))


def _rms(x, gain):
    ms = jnp.mean(x * x, axis=-1, keepdims=True)
    return x * lax.rsqrt(ms + RMS_EPS) * gain


def _rotate(y, lane, half):
    up = pltpu.roll(y, LANES - half, 1)
    dn = pltpu.roll(y, half, 1)
    return jnp.where((lane % (2 * half)) < half, up, dn)


def _qkv_kernel(x_ref, nrm_ref, w_ref, gain_ref, rope_ref, ones_ref, out_ref, h_ref):
    tm = x_ref.shape[0]
    h_ref[...] = _rms(x_ref[...], nrm_ref[...]).astype(BF16)
    lane = lax.broadcasted_iota(jnp.int32, (tm, LANES), 1)
    is_k = lane < HEAD_DIM
    for c in range(QKV_WIDTH // PAIR):
        p = jnp.dot(h_ref[...], w_ref[:, c * PAIR:(c + 1) * PAIR], preferred_element_type=F32)
        ss = jnp.dot((p * p).astype(BF16), ones_ref[...], preferred_element_type=F32)
        inv = lax.rsqrt(ss * (1.0 / HEAD_DIM) + RMS_EPS)
        for hb in range(2):
            blk = 2 * c + hb
            cols = slice(hb * LANES, (hb + 1) * LANES)
            is_q = blk < KV_BLK["a"]
            scale = inv[:, cols] if is_q else jnp.where(is_k, inv[:, cols], 1.0)
            y = p[:, cols] * scale * gain_ref[:, blk * LANES:(blk + 1) * LANES]
            if blk < Q_BLK["b"]:
                tab, half = 0, HEAD_DIM // 4
            elif blk < Q_BLK["c"]:
                tab, half = None, None
            elif is_q:
                tab, half = 4, HEAD_DIM // 2
            elif blk < KV_BLK["b"]:
                tab, half = 2, HEAD_DIM // 4
            elif blk < KV_BLK["c"]:
                tab, half = None, None
            else:
                tab, half = 6, HEAD_DIM // 2
            if tab is not None:
                y = y * rope_ref[tab] + _rotate(y, lane, half) * rope_ref[tab + 1]
            elif is_q:
                y = y * SCALE
            out_ref[:, blk * LANES:(blk + 1) * LANES] = y.astype(BF16)


def _qkv_proj(x, layer, nrm, w_in, gains, rope, ones, tm):
    b, t, d = x.shape
    return pl.pallas_call(
        _qkv_kernel,
        grid=(t // tm, b),
        in_specs=[
            pl.BlockSpec((None, tm, d), lambda i, bb: (bb, i, 0)),
            _resident((None, 1, d), lambda i, bb: (layer, 0, 0)),
            _resident((None, d, QKV_WIDTH), lambda i, bb: (layer, 0, 0)),
            _resident((None, 1, QKV_WIDTH), lambda i, bb: (layer, 0, 0)),
            pl.BlockSpec((8, tm, LANES), lambda i, bb: (0, i, 0)),
            _resident((PAIR, PAIR), lambda i, bb: (0, 0)),
        ],
        out_specs=pl.BlockSpec((None, tm, QKV_WIDTH), lambda i, bb: (bb, i, 0)),
        out_shape=jax.ShapeDtypeStruct((b, t, QKV_WIDTH), BF16),
        scratch_shapes=[pltpu.VMEM((tm, d), BF16)],
        compiler_params=_cparams(2),
        name="qkv_proj",
    )(x, nrm, w_in, gains, rope, ones)


def _stack_heads(q):
    q32 = q.astype(F32)
    lane = lax.broadcasted_iota(jnp.int32, q32.shape, 1)
    lo = lane < HEAD_DIM
    q0 = jnp.where(lo, q32, 0.0)
    q1 = jnp.where(lo, pltpu.roll(q32, HEAD_DIM, 1), 0.0)
    return jnp.concatenate([q0, q1], axis=0).astype(BF16)


def _scores(q2, kv):
    return lax.dot_general(q2, kv, (((1,), (1,)), ((), ())), preferred_element_type=F32)


def _unstack_heads(o2):
    n = o2.shape[0] // 2
    lane = lax.broadcasted_iota(jnp.int32, (n, LANES), 1)
    return jnp.where(lane < HEAD_DIM, pltpu.roll(o2[:n], HEAD_DIM, 1), o2[n:])


def _attn_a_kernel(q_ref, kv_ref, o_ref, m_ref, l_ref, acc_ref, *, tk):
    t = kv_ref.shape[0]
    q2 = _stack_heads(q_ref[...])
    m_ref[...] = jnp.full(m_ref.shape, -jnp.inf, F32)
    l_ref[...] = jnp.zeros(l_ref.shape, F32)
    acc_ref[...] = jnp.zeros(acc_ref.shape, F32)

    def step(j, carry):
        kv = kv_ref[pl.ds(pl.multiple_of(j * tk, tk), tk), :]
        s = _scores(q2, kv)
        m_old = m_ref[...]
        m_new = jnp.maximum(m_old, jnp.max(s, axis=-1, keepdims=True))
        alpha = jnp.exp(m_old - m_new)
        p = jnp.exp(s - m_new[:, :1])
        l_ref[...] = alpha * l_ref[...] + jnp.sum(p, axis=-1, keepdims=True)
        acc_ref[...] = alpha * acc_ref[...] + jnp.dot(
            p.astype(BF16), kv, preferred_element_type=F32)
        m_ref[...] = m_new
        return carry

    lax.fori_loop(0, t // tk, step, 0)
    o_ref[...] = _unstack_heads(acc_ref[...] / l_ref[...])


def _attn_a(qkv, tq, tk):
    b, t, _ = qkv.shape
    return pl.pallas_call(
        functools.partial(_attn_a_kernel, tk=tk),
        grid=(b, A_GROUPS, t // tq),
        in_specs=[
            pl.BlockSpec((None, tq, LANES), lambda bb, g, i: (bb, i, Q_BLK["a"] + g)),
            pl.BlockSpec((None, t, LANES), lambda bb, g, i: (bb, 0, KV_BLK["a"] + g)),
        ],
        out_specs=pl.BlockSpec((None, tq, LANES), lambda bb, g, i: (bb, i, g)),
        out_shape=jax.ShapeDtypeStruct((b, t, A_GROUPS * LANES), F32),
        scratch_shapes=[pltpu.VMEM((2 * tq, LANES), F32)] * 3,
        compiler_params=_cparams(3),
        name="attn_global",
    )(qkv, qkv)


def _attn_b_kernel(q_ref, kv_ref, bias_ref, o_ref):
    rows = q_ref.shape[0] // GRID_W
    n_keys = NA_ROWS * GRID_W

    def row(r, carry):
        r0 = jnp.clip(r - NA_ROWS // 2, 0, rows - NA_ROWS)
        dr = r0 - r + (NA_ROWS - 1)
        q2 = _stack_heads(q_ref[pl.ds(pl.multiple_of(r * GRID_W, GRID_W), GRID_W), :])
        kv = kv_ref[pl.ds(pl.multiple_of(r0 * GRID_W, GRID_W), n_keys), :]
        bias = jnp.concatenate([bias_ref[0, dr], bias_ref[1, dr]], axis=0)
        s = _scores(q2, kv) + bias
        p = jnp.exp(s - jnp.max(s, axis=-1, keepdims=True))
        l = jnp.sum(p, axis=-1, keepdims=True)
        o2 = jnp.dot(p.astype(BF16), kv, preferred_element_type=F32) / l
        o_ref[pl.ds(pl.multiple_of(r * GRID_W, GRID_W), GRID_W), :] = _unstack_heads(o2)
        return carry

    lax.fori_loop(0, rows, row, 0)


def _attn_b(qkv, bias, layer):
    b, t, _ = qkv.shape
    return pl.pallas_call(
        _attn_b_kernel,
        grid=(B_GROUPS, b),
        in_specs=[
            pl.BlockSpec((None, t, LANES), lambda g, bb: (bb, 0, Q_BLK["b"] + g)),
            pl.BlockSpec((None, t, LANES), lambda g, bb: (bb, 0, KV_BLK["b"] + g)),
            pl.BlockSpec((None, None, 2, NA_ROWS, GRID_W, NA_ROWS * GRID_W),
                         lambda g, bb: (layer, g, 0, 0, 0, 0)),
        ],
        out_specs=pl.BlockSpec((None, t, LANES), lambda g, bb: (bb, 0, g)),
        out_shape=jax.ShapeDtypeStruct((b, t, B_GROUPS * LANES), F32),
        compiler_params=_cparams(2),
        name="attn_neighbourhood",
    )(qkv, qkv, bias)


def _attn_band_kernel(q_ref, kv_ref, o_ref, *, tq, n_keys):
    length = kv_ref.shape[0]
    a0 = pl.program_id(3) * tq
    start = jnp.clip(a0 - HALF_WINDOW, 0, length - n_keys)
    start = pl.multiple_of(start, HALF_WINDOW)
    q2 = _stack_heads(q_ref[...])
    kv = kv_ref[pl.ds(start, n_keys), :]
    s = _scores(q2, kv)
    row = lax.broadcasted_iota(jnp.int32, (2 * tq, n_keys), 0)
    qpos = a0 + jnp.where(row >= tq, row - tq, row)
    kpos = start + lax.broadcasted_iota(jnp.int32, (2 * tq, n_keys), 1)
    s = jnp.where(jnp.abs(qpos - kpos) <= HALF_WINDOW, s, NEG_INF)
    m = jnp.max(s, axis=-1, keepdims=True)
    p = jnp.exp(s - m)
    l = jnp.sum(p, axis=-1, keepdims=True)
    o2 = jnp.dot(p.astype(BF16), kv, preferred_element_type=F32) / l
    lse = jnp.broadcast_to(m + jnp.log(l), (2 * tq, LANES))
    lane = lax.broadcasted_iota(jnp.int32, (tq, LANES), 1)
    o_ref[:, :LANES] = _unstack_heads(o2)
    o_ref[:, LANES:] = jnp.where(lane < HEAD_DIM, lse[:tq], lse[tq:])


def _attn_band(qkv, dil, tq):
    b, t, _ = qkv.shape
    length = t // dil
    n_keys = min(length, tq + 2 * HALF_WINDOW)
    view = qkv.reshape(b, length, dil * QKV_WIDTH)
    out = pl.pallas_call(
        functools.partial(_attn_band_kernel, tq=tq, n_keys=n_keys),
        grid=(b, C_GROUPS, dil, length // tq),
        in_specs=[
            pl.BlockSpec((None, tq, LANES),
                         lambda bb, g, r, i: (bb, i, r * N_BLKS + Q_BLK["c"] + g)),
            pl.BlockSpec((None, length, LANES),
                         lambda bb, g, r, i: (bb, 0, r * N_BLKS + KV_BLK["c"] + g)),
        ],
        out_specs=pl.BlockSpec((None, tq, PAIR), lambda bb, g, r, i: (bb, i, r * C_GROUPS + g)),
        out_shape=jax.ShapeDtypeStruct((b, length, dil * C_GROUPS * PAIR), F32),
        compiler_params=_cparams(4),
        name=f"attn_dilated_{dil}",
    )(view, view)
    return out.reshape(b, t, C_GROUPS * PAIR)


def _tail_kernel(x_ref, oa_ref, ob_ref, oc1_ref, oc2_ref, oc3_ref, og_ref, wo_ref, nf_ref,
                 wgu_ref, wd_ref, out_ref, y_ref, act_ref, *, fchunk):
    a_w = A_GROUPS * LANES
    b_w = B_GROUPS * LANES
    y_ref[:, :a_w] = _rms(oa_ref[...], og_ref[:, :a_w]).astype(BF16)
    y_ref[:, a_w:a_w + b_w] = _rms(ob_ref[...], og_ref[:, a_w:a_w + b_w]).astype(BF16)

    merged = []
    for g in range(C_GROUPS):
        outs = [r[:, g * PAIR:g * PAIR + LANES] for r in (oc1_ref, oc2_ref, oc3_ref)]
        lses = [r[:, g * PAIR + LANES:(g + 1) * PAIR] for r in (oc1_ref, oc2_ref, oc3_ref)]
        top = jnp.maximum(jnp.maximum(lses[0], lses[1]), lses[2])
        wts = [jnp.exp(s - top) for s in lses]
        num = wts[0] * outs[0] + wts[1] * outs[1] + wts[2] * outs[2]
        merged.append(num / (wts[0] + wts[1] + wts[2]))
    ssq = sum(jnp.sum(o * o, axis=-1, keepdims=True) for o in merged)
    inv = lax.rsqrt(ssq * (1.0 / (C_GROUPS * LANES)) + RMS_EPS)
    for g in range(C_GROUPS):
        cols = slice(a_w + b_w + g * LANES, a_w + b_w + (g + 1) * LANES)
        y_ref[:, cols] = (merged[g] * inv * og_ref[:, cols]).astype(BF16)

    x1 = x_ref[...] + jnp.dot(y_ref[...], wo_ref[...], preferred_element_type=F32)
    h = _rms(x1, nf_ref[...]).astype(BF16)
    for c in range(FFN_DIM // fchunk):
        gate = jnp.dot(h, wgu_ref[:, c * fchunk:(c + 1) * fchunk], preferred_element_type=F32)
        up = jnp.dot(h, wgu_ref[:, FFN_DIM + c * fchunk:FFN_DIM + (c + 1) * fchunk],
                     preferred_element_type=F32)
        act_ref[:, c * fchunk:(c + 1) * fchunk] = (jax.nn.silu(gate) * up).astype(BF16)
    out_ref[...] = x1 + jnp.dot(act_ref[...], wd_ref[...], preferred_element_type=F32)


def _tail(x, oa, ob, ocs, layer, out_gain, w_out, norm_ffn, w_gu, w_down, tm, fchunk):
    b, t, d = x.shape
    n = b * t
    flat = lambda a: a.reshape(n, a.shape[-1])
    row = lambda w: pl.BlockSpec((tm, w), lambda i: (i, 0))
    out = pl.pallas_call(
        functools.partial(_tail_kernel, fchunk=fchunk),
        grid=(n // tm,),
        in_specs=[
            row(d), row(A_GROUPS * LANES), row(B_GROUPS * LANES),
            row(C_GROUPS * PAIR), row(C_GROUPS * PAIR), row(C_GROUPS * PAIR),
            _resident((None, 1, d), lambda i: (layer, 0, 0)),
            _resident((None, d, d), lambda i: (layer, 0, 0)),
            _resident((None, 1, d), lambda i: (layer, 0, 0)),
            _resident((None, d, 2 * FFN_DIM), lambda i: (layer, 0, 0)),
            _resident((None, FFN_DIM, d), lambda i: (layer, 0, 0)),
        ],
        out_specs=row(d),
        out_shape=jax.ShapeDtypeStruct((n, d), F32),
        scratch_shapes=[pltpu.VMEM((tm, d), BF16), pltpu.VMEM((tm, FFN_DIM), BF16)],
        compiler_params=_cparams(1),
        name="out_proj_ffn",
    )(flat(x), flat(oa), flat(ob), *[flat(o) for o in ocs],
      out_gain, w_out, norm_ffn, w_gu, w_down)
    return out.reshape(b, t, d)


def _rope_tables(pos, dim):
    inv_freq = ROPE_THETA ** (-jnp.arange(0, dim, 2, dtype=F32) / dim)
    ang = pos[:, None] * inv_freq[None, :]
    ang = jnp.concatenate([ang, ang], axis=-1)
    return jnp.cos(ang), jnp.sin(ang)


def _rope_stack(t):
    pos = jnp.arange(t, dtype=jnp.int32)
    half = HEAD_DIM // 2
    cos_r, sin_r = _rope_tables((pos // GRID_W).astype(F32), half)
    cos_c, sin_c = _rope_tables((pos % GRID_W).astype(F32), half)
    cos_1, sin_1 = _rope_tables(pos.astype(F32), HEAD_DIM)
    j = jnp.arange(HEAD_DIM)
    cos_a = jnp.concatenate([cos_r, cos_c], axis=-1)
    sin_a = jnp.concatenate([sin_r, sin_c], axis=-1) * jnp.where(j % half < half // 2, -1.0, 1.0)
    sin_1 = sin_1 * jnp.where(j < half, -1.0, 1.0)
    one, zero = jnp.ones_like(cos_a), jnp.zeros_like(cos_a)
    cat = lambda u, v: jnp.concatenate([u, v], axis=-1)
    return jnp.stack([
        cat(cos_a, cos_a) * SCALE, cat(sin_a, sin_a) * SCALE, cat(cos_a, one), cat(sin_a, zero),
        cat(cos_1, cos_1) * SCALE, cat(sin_1, sin_1) * SCALE, cat(cos_1, one), cat(sin_1, zero),
    ]).astype(F32)


def _neighbourhood_bias(rpb, t):
    depth = rpb.shape[0]
    qc = jnp.arange(GRID_W)[:, None]
    kc = jnp.arange(GRID_W)[None, :]
    c0 = jnp.clip(qc - NA_COLS // 2, 0, GRID_W - NA_COLS)
    valid = (kc >= c0) & (kc < c0 + NA_COLS)
    dc = jnp.clip(kc - qc + NA_COLS - 1, 0, 2 * NA_COLS - 2)
    dr = jnp.arange(NA_ROWS)[:, None] + jnp.arange(NA_ROWS)[None, :]
    tab = rpb[:, :, dr][:, :, :, :, dc]
    tab = jnp.where(valid[None, None, None, None], tab, NEG_INF)
    tab = jnp.transpose(tab, (0, 1, 2, 4, 3, 5))
    return tab.reshape(depth, B_GROUPS, 2, NA_ROWS, GRID_W, NA_ROWS * GRID_W).astype(F32)


def _prepare(norm_mix, w_in, q_gain, k_gain, rpb, out_gain, w_out, norm_ffn, w_gate_up, w_down):
    depth = w_in.shape[0]
    q_w = N_HEADS * HEAD_DIM
    kv_w = KV_HEADS * HEAD_DIM
    wq = w_in[..., :q_w]
    wk = w_in[..., q_w:q_w + kv_w].reshape(depth, D_MODEL, KV_HEADS, 1, HEAD_DIM)
    wv = w_in[..., q_w + kv_w:].reshape(depth, D_MODEL, KV_HEADS, 1, HEAD_DIM)
    wkv = jnp.concatenate([wk, wv], axis=3).reshape(depth, D_MODEL, 2 * kv_w)
    w_qkv = jnp.concatenate([wq, wkv], axis=-1).astype(BF16)

    heads_q = (4, 6, 6)
    heads_kv = (2, 3, 3)
    gq = jnp.concatenate([jnp.tile(q_gain[:, m], (1, heads_q[m])) for m in range(3)], axis=-1)
    ones = jnp.ones((depth, HEAD_DIM), F32)
    gkv = jnp.concatenate(
        [jnp.tile(jnp.concatenate([k_gain[:, m], ones], axis=-1), (1, heads_kv[m]))
         for m in range(3)], axis=-1)
    gains = jnp.concatenate([gq, gkv], axis=-1).reshape(depth, 1, QKV_WIDTH).astype(F32)

    seg = jnp.arange(PAIR) // HEAD_DIM
    head_ones = (seg[:, None] == seg[None, :]).astype(BF16)
    return dict(
        norm_mix=norm_mix.reshape(depth, 1, D_MODEL), w_qkv=w_qkv, gains=gains,
        head_ones=head_ones, out_gain=out_gain.reshape(depth, 1, D_MODEL),
        w_out=w_out.astype(BF16), norm_ffn=norm_ffn.reshape(depth, 1, D_MODEL),
        w_gu=w_gate_up.astype(BF16), w_down=w_down.astype(BF16), rpb=rpb)


def _trunk(x, prm, *, tm_qkv=512, tq_a=256, tk_a=512, tq_c=128, tm_tail=256, fchunk=256):
    b, t, _ = x.shape
    depth = prm["w_qkv"].shape[0]
    rope = _rope_stack(t)
    bias = _neighbourhood_bias(prm["rpb"], t)
    for layer in range(depth):
        qkv = _qkv_proj(x, layer, prm["norm_mix"], prm["w_qkv"], prm["gains"], rope,
                        prm["head_ones"], tm_qkv)
        oa = _attn_a(qkv, tq_a, tk_a)
        ob = _attn_b(qkv, bias, layer)
        ocs = [_attn_band(qkv, dil, tq_c) for dil in DILATIONS]
        x = _tail(x, oa, ob, ocs, layer, prm["out_gain"], prm["w_out"], prm["norm_ffn"],
                  prm["w_gu"], prm["w_down"], tm_tail, fchunk)
    return x


def kernel(x_prompt, x_sample, norm_mix, w_in, q_gain, k_gain, rpb, out_gain, w_out, norm_ffn,
           w_gate_up, w_down):
    prm = _prepare(norm_mix, w_in, q_gain, k_gain, rpb, out_gain, w_out, norm_ffn, w_gate_up,
                   w_down)
    return (_trunk(x_prompt, prm), _trunk(x_sample, prm))
```

```python
import functools

import jax
import jax.numpy as jnp
from jax import lax
from jax.experimental import pallas as pl
from jax.experimental.pallas import tpu as pltpu

F32 = jnp.float32
BF16 = jnp.bfloat16

D_MODEL = 1024
HEAD_DIM = 64
N_HEADS = 16
KV_HEADS = 8
QKV_WIDTH = 2048
FFN_DIM = 2816
GRID_W = 64
NA_ROWS = 8
NA_COLS = 16
DILATIONS = (1, 4, 16)
HALF_WINDOW = 64
ROPE_THETA = 10000.0
RMS_EPS = 1e-6
NEG_INF = -1e30
SCALE = HEAD_DIM ** -0.5

LANES = 128
PAIR = 2 * LANES
A_GROUPS, B_GROUPS, C_GROUPS = 2, 3, 3
Q_BLK = {"a": 0, "b": 2, "c": 5}
KV_BLK = {"a": 8, "b": 10, "c": 13}
N_BLKS = QKV_WIDTH // LANES

VMEM_LIMIT = 56 * 1024 * 1024


def _cparams(n_axes):
    return pltpu.CompilerParams(
        dimension_semantics=("arbitrary",) * n_axes, vmem_limit_bytes=VMEM_LIMIT)


def _resident(block_shape, index_map):
    return pl.BlockSpec(block_shape, index_map, pipeline_mode=pl.Buffered(1))


def _rms(x, gain):
    ms = jnp.mean(x * x, axis=-1, keepdims=True)
    return x * lax.rsqrt(ms + RMS_EPS) * gain


def _rotate(y, lane, half):
    up = pltpu.roll(y, LANES - half, 1)
    dn = pltpu.roll(y, half, 1)
    return jnp.where((lane % (2 * half)) < half, up, dn)


def _qkv_kernel(x_ref, nrm_ref, w_ref, gain_ref, rope_ref, ones_ref, out_ref, h_ref):
    tm = x_ref.shape[0]
    h_ref[...] = _rms(x_ref[...], nrm_ref[...]).astype(BF16)
    lane = lax.broadcasted_iota(jnp.int32, (tm, LANES), 1)
    is_k = lane < HEAD_DIM
    for c in range(QKV_WIDTH // PAIR):
        p = jnp.dot(h_ref[...], w_ref[:, c * PAIR:(c + 1) * PAIR], preferred_element_type=F32)
        ss = jnp.dot((p * p).astype(BF16), ones_ref[...], preferred_element_type=F32)
        inv = lax.rsqrt(ss * (1.0 / HEAD_DIM) + RMS_EPS)
        for hb in range(2):
            blk = 2 * c + hb
            cols = slice(hb * LANES, (hb + 1) * LANES)
            is_q = blk < KV_BLK["a"]
            scale = inv[:, cols] if is_q else jnp.where(is_k, inv[:, cols], 1.0)
            y = p[:, cols] * scale * gain_ref[:, blk * LANES:(blk + 1) * LANES]
            if blk < Q_BLK["b"]:
                tab, half = 0, HEAD_DIM // 4
            elif blk < Q_BLK["c"]:
                tab, half = None, None
            elif is_q:
                tab, half = 4, HEAD_DIM // 2
            elif blk < KV_BLK["b"]:
                tab, half = 2, HEAD_DIM // 4
            elif blk < KV_BLK["c"]:
                tab, half = None, None
            else:
                tab, half = 6, HEAD_DIM // 2
            if tab is not None:
                y = y * rope_ref[tab] + _rotate(y, lane, half) * rope_ref[tab + 1]
            elif is_q:
                y = y * SCALE
            out_ref[:, blk * LANES:(blk + 1) * LANES] = y.astype(BF16)


def _qkv_proj(x, layer, nrm, w_in, gains, rope, ones, tm):
    b, t, d = x.shape
    return pl.pallas_call(
        _qkv_kernel,
        grid=(t // tm, b),
        in_specs=[
            pl.BlockSpec((None, tm, d), lambda i, bb: (bb, i, 0)),
            _resident((None, 1, d), lambda i, bb: (layer, 0, 0)),
            _resident((None, d, QKV_WIDTH), lambda i, bb: (layer, 0, 0)),
            _resident((None, 1, QKV_WIDTH), lambda i, bb: (layer, 0, 0)),
            pl.BlockSpec((8, tm, LANES), lambda i, bb: (0, i, 0)),
            _resident((PAIR, PAIR), lambda i, bb: (0, 0)),
        ],
        out_specs=pl.BlockSpec((None, tm, QKV_WIDTH), lambda i, bb: (bb, i, 0)),
        out_shape=jax.ShapeDtypeStruct((b, t, QKV_WIDTH), BF16),
        scratch_shapes=[pltpu.VMEM((tm, d), BF16)],
        compiler_params=_cparams(2),
        name="qkv_proj",
    )(x, nrm, w_in, gains, rope, ones)


def _stack_heads(q):
    q32 = q.astype(F32)
    lane = lax.broadcasted_iota(jnp.int32, q32.shape, 1)
    lo = lane < HEAD_DIM
    q0 = jnp.where(lo, q32, 0.0)
    q1 = jnp.where(lo, pltpu.roll(q32, HEAD_DIM, 1), 0.0)
    return jnp.concatenate([q0, q1], axis=0).astype(BF16)


def _scores(a, b):
    return lax.dot_general(a, b, (((1,), (1,)), ((), ())), preferred_element_type=F32)


def _ones_for_k(kv32):
    lane = lax.broadcasted_iota(jnp.int32, kv32.shape, 1)
    return jnp.where(lane < HEAD_DIM, 1.0, kv32).astype(BF16)


def _split_pv(o2):
    n = o2.shape[0] // 2
    lo = lax.broadcasted_iota(jnp.int32, (n, LANES), 1) < HEAD_DIM
    pv = jnp.where(lo, pltpu.roll(o2[:n], HEAD_DIM, 1), o2[n:])
    den = jnp.where(lo, o2[:n], pltpu.roll(o2[n:], HEAD_DIM, 1))
    return pv, den


def _attn_a_kernel(q_ref, kv_ref, o_ref, kv1_ref, s_ref, *, tk):
    t = kv_ref.shape[0]
    w = 2 * q_ref.shape[0]

    @pl.when(pl.program_id(2) == 0)
    def _():
        kv1_ref[...] = _ones_for_k(kv_ref[...].astype(F32))

    q2 = _stack_heads(q_ref[...])

    run_max = jnp.full((w, LANES), -jnp.inf, F32)
    for j in range(t // tk):
        sc = _scores(q2, kv_ref[j * tk:(j + 1) * tk, :])
        s_ref[:, j * tk:(j + 1) * tk] = sc
        for c in range(tk // LANES):
            run_max = jnp.maximum(run_max, sc[:, c * LANES:(c + 1) * LANES])
    m = jnp.max(run_max, axis=-1, keepdims=True)

    acc = jnp.zeros((w, LANES), F32)
    for j in range(t // tk):
        p = jnp.exp(s_ref[:, j * tk:(j + 1) * tk] - m).astype(BF16)
        acc = acc + jnp.dot(p, kv1_ref[j * tk:(j + 1) * tk, :], preferred_element_type=F32)
    pv, den = _split_pv(acc)
    o_ref[...] = pv / den


def _attn_a(qkv, tq, tk):
    b, t, _ = qkv.shape
    return pl.pallas_call(
        functools.partial(_attn_a_kernel, tk=tk),
        grid=(b, A_GROUPS, t // tq),
        in_specs=[
            pl.BlockSpec((None, tq, LANES), lambda bb, g, i: (bb, i, Q_BLK["a"] + g)),
            pl.BlockSpec((None, t, LANES), lambda bb, g, i: (bb, 0, KV_BLK["a"] + g)),
        ],
        out_specs=pl.BlockSpec((None, tq, LANES), lambda bb, g, i: (bb, i, g)),
        out_shape=jax.ShapeDtypeStruct((b, t, A_GROUPS * LANES), F32),
        scratch_shapes=[pltpu.VMEM((t, LANES), BF16), pltpu.VMEM((2 * tq, t), F32)],
        compiler_params=_cparams(3),
        name="attn_global",
    )(qkv, qkv)


def _attn_b_kernel(q_ref, kv_ref, bias_ref, o_ref, kv1_ref, *, unroll):
    rows = q_ref.shape[0] // GRID_W
    n_keys = NA_ROWS * GRID_W
    kv1_ref[...] = _ones_for_k(kv_ref[...].astype(F32))

    def body(it, carry):
        offs, scores = [], []
        for u in range(unroll):
            r = it * unroll + u
            r0 = jnp.clip(r - NA_ROWS // 2, 0, rows - NA_ROWS)
            qoff = pl.multiple_of(r * GRID_W, GRID_W)
            koff = pl.multiple_of(r0 * GRID_W, GRID_W)
            q2 = _stack_heads(q_ref[pl.ds(qoff, GRID_W), :])
            scores.append(_scores(q2, kv_ref[pl.ds(koff, n_keys), :])
                          + bias_ref[r0 - r + (NA_ROWS - 1)])
            offs.append((qoff, koff))
        probs = [jnp.exp(sc - jnp.max(sc, axis=-1, keepdims=True)).astype(BF16) for sc in scores]
        outs = [jnp.dot(p, kv1_ref[pl.ds(koff, n_keys), :], preferred_element_type=F32)
                for p, (_, koff) in zip(probs, offs)]
        for o2, (qoff, _) in zip(outs, offs):
            pv, den = _split_pv(o2)
            o_ref[pl.ds(qoff, GRID_W), :] = pv / den
        return carry

    lax.fori_loop(0, rows // unroll, body, 0)


def _attn_b(qkv, bias, layer, unroll):
    b, t, _ = qkv.shape
    return pl.pallas_call(
        functools.partial(_attn_b_kernel, unroll=unroll),
        grid=(B_GROUPS, b),
        in_specs=[
            pl.BlockSpec((None, t, LANES), lambda g, bb: (bb, 0, Q_BLK["b"] + g)),
            pl.BlockSpec((None, t, LANES), lambda g, bb: (bb, 0, KV_BLK["b"] + g)),
            pl.BlockSpec((None, None, NA_ROWS, 2 * GRID_W, NA_ROWS * GRID_W),
                         lambda g, bb: (layer, g, 0, 0, 0)),
        ],
        out_specs=pl.BlockSpec((None, t, LANES), lambda g, bb: (bb, 0, g)),
        out_shape=jax.ShapeDtypeStruct((b, t, B_GROUPS * LANES), F32),
        scratch_shapes=[pltpu.VMEM((t, LANES), BF16)],
        compiler_params=_cparams(2),
        name="attn_neighbourhood",
    )(qkv, qkv, bias)


def _attn_c_kernel(q_ref, kv_ref, bias_ref, o_ref, qf_ref, kvf_ref, qs_ref, kvs_ref, kv1_ref,
                   m_ref, w_ref, n_ref, *, tq, unroll):
    t = q_ref.shape[0]
    lo = lax.broadcasted_iota(jnp.int32, (tq, LANES), 1) < HEAD_DIM
    qf_ref[...] = q_ref[...].astype(F32)
    kvf_ref[...] = kv_ref[...].astype(F32)

    for branch, dil in enumerate(DILATIONS):
        length = t // dil
        n_tiles = length // tq
        n_keys = min(length, tq + 2 * HALF_WINDOW)
        first, last = branch == 0, branch == len(DILATIONS) - 1

        if dil == 1:
            q_src, kv_src = q_ref, kv_ref
            kv1_ref[...] = _ones_for_k(kvf_ref[...])
        else:
            q_src, kv_src = qs_ref, kvs_ref

            def gather_class(r, carry, dil=dil, length=length):
                rows = pl.ds(pl.multiple_of(r * length, LANES), length)
                qs_ref[rows, :] = qf_ref[pl.ds(r, length, stride=dil), :].astype(BF16)
                kvd = kvf_ref[pl.ds(r, length, stride=dil), :]
                kvs_ref[rows, :] = kvd.astype(BF16)
                kv1_ref[rows, :] = _ones_for_k(kvd)
                return carry

            lax.fori_loop(0, dil, gather_class, 0)

        def body(it, carry, dil=dil, length=length, n_tiles=n_tiles, n_keys=n_keys,
                 first=first, last=last, q_src=q_src, kv_src=kv_src):
            where, scores = [], []
            for u in range(unroll):
                g = it * unroll + u
                r = g // n_tiles
                i = g - r * n_tiles
                a0 = i * tq
                start = jnp.clip(a0 - HALF_WINDOW, 0, length - n_keys)
                kind = jnp.where(i == 0, 0, jnp.where(i == n_tiles - 1, 2, 1))
                qrow = pl.multiple_of(g * tq, tq)
                krow = pl.multiple_of(r * length + start, HALF_WINDOW)
                q2 = _stack_heads(q_src[pl.ds(qrow, tq), :])
                scores.append(_scores(q2, kv_src[pl.ds(krow, n_keys), :])
                              + bias_ref[kind, :, :n_keys])
                tok = pl.ds(qrow, tq) if dil == 1 else pl.ds(dil * a0 + r, tq, stride=dil)
                where.append((krow, tok))
            tops = [jnp.max(sc, axis=-1, keepdims=True) for sc in scores]
            probs = [jnp.exp(sc - m).astype(BF16) for sc, m in zip(scores, tops)]
            outs = [jnp.dot(p, kv1_ref[pl.ds(krow, n_keys), :], preferred_element_type=F32)
                    for p, (krow, _) in zip(probs, where)]
            for o2, m, (_, tok) in zip(outs, tops, where):
                pv, den = _split_pv(o2)
                top = jnp.where(lo, jnp.broadcast_to(m[:tq], (tq, LANES)),
                                jnp.broadcast_to(m[tq:], (tq, LANES)))
                if first:
                    m_ref[tok, :] = top
                    w_ref[tok, :] = den
                    n_ref[tok, :] = pv
                else:
                    m_old = m_ref[tok, :]
                    m_new = jnp.maximum(m_old, top)
                    e_old = jnp.exp(m_old - m_new)
                    e_new = jnp.exp(top - m_new)
                    num = e_old * n_ref[tok, :] + e_new * pv
                    wsum = e_old * w_ref[tok, :] + e_new * den
                    if last:
                        o_ref[tok, :] = num / wsum
                    else:
                        m_ref[tok, :] = m_new
                        w_ref[tok, :] = wsum
                        n_ref[tok, :] = num
            return carry

        lax.fori_loop(0, (t // tq) // unroll, body, 0)


def _band_bias(tq):
    row = jnp.arange(2 * tq)[:, None] % tq
    col = jnp.arange(tq + 2 * HALF_WINDOW)[None, :]
    delta = jnp.arange(3)[:, None, None] * HALF_WINDOW
    return jnp.where(jnp.abs(delta + row - col) <= HALF_WINDOW, 0.0, NEG_INF).astype(F32)


def _attn_c(qkv, band, tq, unroll):
    b, t, _ = qkv.shape
    f32 = pltpu.VMEM((t, LANES), F32)
    b16 = pltpu.VMEM((t, LANES), BF16)
    return pl.pallas_call(
        functools.partial(_attn_c_kernel, tq=tq, unroll=unroll),
        grid=(C_GROUPS, b),
        in_specs=[
            pl.BlockSpec((None, t, LANES), lambda g, bb: (bb, 0, Q_BLK["c"] + g)),
            pl.BlockSpec((None, t, LANES), lambda g, bb: (bb, 0, KV_BLK["c"] + g)),
            _resident(band.shape, lambda g, bb: (0, 0, 0)),
        ],
        out_specs=pl.BlockSpec((None, t, LANES), lambda g, bb: (bb, 0, g)),
        out_shape=jax.ShapeDtypeStruct((b, t, C_GROUPS * LANES), F32),
        scratch_shapes=[f32, f32, b16, b16, b16, f32, f32, f32],
        compiler_params=_cparams(2),
        name="attn_dilated",
    )(qkv, qkv, band)


def _tail_kernel(x_ref, oa_ref, ob_ref, oc_ref, og_ref, wo_ref, nf_ref, wgu_ref, wd_ref,
                 out_ref, y_ref, act_ref, *, fchunk):
    a_w = A_GROUPS * LANES
    b_w = B_GROUPS * LANES
    y_ref[:, :a_w] = _rms(oa_ref[...], og_ref[:, :a_w]).astype(BF16)
    y_ref[:, a_w:a_w + b_w] = _rms(ob_ref[...], og_ref[:, a_w:a_w + b_w]).astype(BF16)
    y_ref[:, a_w + b_w:] = _rms(oc_ref[...], og_ref[:, a_w + b_w:]).astype(BF16)

    x1 = x_ref[...] + jnp.dot(y_ref[...], wo_ref[...], preferred_element_type=F32)
    h = _rms(x1, nf_ref[...]).astype(BF16)
    for c in range(FFN_DIM // fchunk):
        gate = jnp.dot(h, wgu_ref[:, c * fchunk:(c + 1) * fchunk], preferred_element_type=F32)
        up = jnp.dot(h, wgu_ref[:, FFN_DIM + c * fchunk:FFN_DIM + (c + 1) * fchunk],
                     preferred_element_type=F32)
        act_ref[:, c * fchunk:(c + 1) * fchunk] = (jax.nn.silu(gate) * up).astype(BF16)
    out_ref[...] = x1 + jnp.dot(act_ref[...], wd_ref[...], preferred_element_type=F32)


def _tail(x, oa, ob, oc, layer, out_gain, w_out, norm_ffn, w_gu, w_down, tm, fchunk):
    b, t, d = x.shape
    n = b * t
    flat = lambda a: a.reshape(n, a.shape[-1])
    row = lambda w: pl.BlockSpec((tm, w), lambda i: (i, 0))
    out = pl.pallas_call(
        functools.partial(_tail_kernel, fchunk=fchunk),
        grid=(n // tm,),
        in_specs=[
            row(d), row(A_GROUPS * LANES), row(B_GROUPS * LANES), row(C_GROUPS * LANES),
            _resident((None, 1, d), lambda i: (layer, 0, 0)),
            _resident((None, d, d), lambda i: (layer, 0, 0)),
            _resident((None, 1, d), lambda i: (layer, 0, 0)),
            _resident((None, d, 2 * FFN_DIM), lambda i: (layer, 0, 0)),
            _resident((None, FFN_DIM, d), lambda i: (layer, 0, 0)),
        ],
        out_specs=row(d),
        out_shape=jax.ShapeDtypeStruct((n, d), F32),
        scratch_shapes=[pltpu.VMEM((tm, d), BF16), pltpu.VMEM((tm, FFN_DIM), BF16)],
        compiler_params=_cparams(1),
        name="out_proj_ffn",
    )(flat(x), flat(oa), flat(ob), flat(oc),
      out_gain, w_out, norm_ffn, w_gu, w_down)
    return out.reshape(b, t, d)


def _rope_tables(pos, dim):
    inv_freq = ROPE_THETA ** (-jnp.arange(0, dim, 2, dtype=F32) / dim)
    ang = pos[:, None] * inv_freq[None, :]
    ang = jnp.concatenate([ang, ang], axis=-1)
    return jnp.cos(ang), jnp.sin(ang)


def _rope_stack(t):
    pos = jnp.arange(t, dtype=jnp.int32)
    half = HEAD_DIM // 2
    cos_r, sin_r = _rope_tables((pos // GRID_W).astype(F32), half)
    cos_c, sin_c = _rope_tables((pos % GRID_W).astype(F32), half)
    cos_1, sin_1 = _rope_tables(pos.astype(F32), HEAD_DIM)
    j = jnp.arange(HEAD_DIM)
    cos_a = jnp.concatenate([cos_r, cos_c], axis=-1)
    sin_a = jnp.concatenate([sin_r, sin_c], axis=-1) * jnp.where(j % half < half // 2, -1.0, 1.0)
    sin_1 = sin_1 * jnp.where(j < half, -1.0, 1.0)
    one, zero = jnp.ones_like(cos_a), jnp.zeros_like(cos_a)
    cat = lambda u, v: jnp.concatenate([u, v], axis=-1)
    return jnp.stack([
        cat(cos_a, cos_a) * SCALE, cat(sin_a, sin_a) * SCALE, cat(cos_a, one), cat(sin_a, zero),
        cat(cos_1, cos_1) * SCALE, cat(sin_1, sin_1) * SCALE, cat(cos_1, one), cat(sin_1, zero),
    ]).astype(F32)


def _neighbourhood_bias(rpb, t):
    depth = rpb.shape[0]
    qc = jnp.arange(GRID_W)[:, None]
    kc = jnp.arange(GRID_W)[None, :]
    c0 = jnp.clip(qc - NA_COLS // 2, 0, GRID_W - NA_COLS)
    valid = (kc >= c0) & (kc < c0 + NA_COLS)
    dc = jnp.clip(kc - qc + NA_COLS - 1, 0, 2 * NA_COLS - 2)
    dr = jnp.arange(NA_ROWS)[:, None] + jnp.arange(NA_ROWS)[None, :]
    tab = rpb[:, :, dr][:, :, :, :, dc]
    tab = jnp.where(valid[None, None, None, None], tab, NEG_INF)
    tab = jnp.transpose(tab, (0, 1, 2, 4, 3, 5))
    tab = tab.reshape(depth, B_GROUPS, 2, NA_ROWS, GRID_W, NA_ROWS * GRID_W)
    tab = jnp.transpose(tab, (0, 1, 3, 2, 4, 5))
    return tab.reshape(depth, B_GROUPS, NA_ROWS, 2 * GRID_W, NA_ROWS * GRID_W).astype(F32)


def _prepare(norm_mix, w_in, q_gain, k_gain, rpb, out_gain, w_out, norm_ffn, w_gate_up, w_down):
    depth = w_in.shape[0]
    q_w = N_HEADS * HEAD_DIM
    kv_w = KV_HEADS * HEAD_DIM
    wq = w_in[..., :q_w]
    wk = w_in[..., q_w:q_w + kv_w].reshape(depth, D_MODEL, KV_HEADS, 1, HEAD_DIM)
    wv = w_in[..., q_w + kv_w:].reshape(depth, D_MODEL, KV_HEADS, 1, HEAD_DIM)
    wkv = jnp.concatenate([wk, wv], axis=3).reshape(depth, D_MODEL, 2 * kv_w)
    w_qkv = jnp.concatenate([wq, wkv], axis=-1).astype(BF16)

    heads_q = (4, 6, 6)
    heads_kv = (2, 3, 3)
    gq = jnp.concatenate([jnp.tile(q_gain[:, m], (1, heads_q[m])) for m in range(3)], axis=-1)
    ones = jnp.ones((depth, HEAD_DIM), F32)
    gkv = jnp.concatenate(
        [jnp.tile(jnp.concatenate([k_gain[:, m], ones], axis=-1), (1, heads_kv[m]))
         for m in range(3)], axis=-1)
    gains = jnp.concatenate([gq, gkv], axis=-1).reshape(depth, 1, QKV_WIDTH).astype(F32)

    seg = jnp.arange(PAIR) // HEAD_DIM
    head_ones = (seg[:, None] == seg[None, :]).astype(BF16)
    return dict(
        norm_mix=norm_mix.reshape(depth, 1, D_MODEL), w_qkv=w_qkv, gains=gains,
        head_ones=head_ones, out_gain=out_gain.reshape(depth, 1, D_MODEL),
        w_out=w_out.astype(BF16), norm_ffn=norm_ffn.reshape(depth, 1, D_MODEL),
        w_gu=w_gate_up.astype(BF16), w_down=w_down.astype(BF16), rpb=rpb)


def _trunk(x, prm, *, tm_qkv=512, tq_a=128, tk_a=512, tq_c=128, unroll_b=8, unroll_c=4,
           tm_tail=256, fchunk=256):
    b, t, _ = x.shape
    depth = prm["w_qkv"].shape[0]
    rope = _rope_stack(t)
    bias = _neighbourhood_bias(prm["rpb"], t)
    band = _band_bias(tq_c)
    for layer in range(depth):
        qkv = _qkv_proj(x, layer, prm["norm_mix"], prm["w_qkv"], prm["gains"], rope,
                        prm["head_ones"], tm_qkv)
        oa = _attn_a(qkv, tq_a, tk_a)
        ob = _attn_b(qkv, bias, layer, unroll_b)
        oc = _attn_c(qkv, band, tq_c, unroll_c)
        x = _tail(x, oa, ob, oc, layer, prm["out_gain"], prm["w_out"], prm["norm_ffn"],
                  prm["w_gu"], prm["w_down"], tm_tail, fchunk)
    return x


def kernel(x_prompt, x_sample, norm_mix, w_in, q_gain, k_gain, rpb, out_gain, w_out, norm_ffn,
           w_gate_up, w_down):
    prm = _prepare(norm_mix, w_in, q_gain, k_gain, rpb, out_gain, w_out, norm_ffn, w_gate_up,
                   w_down)
    return (_trunk(x_prompt, prm), _trunk(x_sample, prm))
```

```python
import functools

import jax
import jax.numpy as jnp
from jax import lax
from jax.experimental import pallas as pl
from jax.experimental.pallas import tpu as pltpu

F32 = jnp.float32
BF16 = jnp.bfloat16

D_MODEL = 1024
HEAD_DIM = 64
N_HEADS = 16
KV_HEADS = 8
QKV_WIDTH = 2048
FFN_DIM = 2816
GRID_W = 64
NA_ROWS = 8
NA_COLS = 16
DILATIONS = (1, 4, 16)
HALF_WINDOW = 64
TQ_C = 2 * HALF_WINDOW
ROPE_THETA = 10000.0
RMS_EPS = 1e-6
NEG_INF = -1e30
LOG2E = 1.4426950408889634
SCALE = HEAD_DIM ** -0.5 * LOG2E

LANES = 128
PAIR = 2 * LANES
A_GROUPS, B_GROUPS, C_GROUPS = 2, 3, 3
Q_BLK = {"a": 0, "b": 2, "c": 5}
KV_BLK = {"a": 8, "b": 10, "c": 13}
N_BLKS = QKV_WIDTH // LANES

VMEM_LIMIT = 56 * 1024 * 1024


def _cparams(n_axes):
    return pltpu.CompilerParams(
        dimension_semantics=("arbitrary",) * n_axes, vmem_limit_bytes=VMEM_LIMIT)


def _resident(block_shape, index_map):
    return pl.BlockSpec(block_shape, index_map, pipeline_mode=pl.Buffered(1))


def _rms(x, gain):
    ms = jnp.mean(x * x, axis=-1, keepdims=True)
    return x * lax.rsqrt(ms + RMS_EPS) * gain


def _rotate(y, lane, half):
    up = pltpu.roll(y, LANES - half, 1)
    dn = pltpu.roll(y, half, 1)
    return jnp.where((lane % (2 * half)) < half, up, dn)


def _qkv_kernel(x_ref, nrm_ref, w_ref, gain_ref, rope_ref, ones_ref, out_ref, h_ref, *, chunk):
    tm = x_ref.shape[0]
    h_ref[...] = _rms(x_ref[...], nrm_ref[...]).astype(BF16)
    lane = lax.broadcasted_iota(jnp.int32, (tm, LANES), 1)
    is_k = lane < HEAD_DIM
    for c in range(QKV_WIDTH // chunk):
        wide = jnp.dot(h_ref[...], w_ref[:, c * chunk:(c + 1) * chunk], preferred_element_type=F32)
        for hb in range(chunk // LANES):
            if hb % 2 == 0:
                p = wide[:, hb * LANES:hb * LANES + PAIR]
                ss = jnp.dot((p * p).astype(BF16), ones_ref[...], preferred_element_type=F32)
                inv = lax.rsqrt(ss * (1.0 / HEAD_DIM) + RMS_EPS)
            blk = c * (chunk // LANES) + hb
            cols = slice((hb % 2) * LANES, (hb % 2 + 1) * LANES)
            is_q = blk < KV_BLK["a"]
            scale = inv[:, cols] if is_q else jnp.where(is_k, inv[:, cols], 1.0)
            y = p[:, cols] * scale * gain_ref[:, blk * LANES:(blk + 1) * LANES]
            if blk < Q_BLK["b"]:
                tab, half = 0, HEAD_DIM // 4
            elif blk < Q_BLK["c"]:
                tab, half = None, None
            elif is_q:
                tab, half = 4, HEAD_DIM // 2
            elif blk < KV_BLK["b"]:
                tab, half = 2, HEAD_DIM // 4
            elif blk < KV_BLK["c"]:
                tab, half = None, None
            else:
                tab, half = 6, HEAD_DIM // 2
            if tab is not None:
                y = y * rope_ref[tab] + _rotate(y, lane, half) * rope_ref[tab + 1]
            elif is_q:
                y = y * SCALE
            out_ref[:, blk * LANES:(blk + 1) * LANES] = y.astype(BF16)


def _qkv_proj(x, layer, nrm, w_in, gains, rope, ones, tm):
    b, t, d = x.shape
    return pl.pallas_call(
        functools.partial(_qkv_kernel, chunk=4 * LANES),
        grid=(t // tm, b),
        in_specs=[
            pl.BlockSpec((None, tm, d), lambda i, bb: (bb, i, 0)),
            _resident((None, 1, d), lambda i, bb: (layer, 0, 0)),
            _resident((None, d, QKV_WIDTH), lambda i, bb: (layer, 0, 0)),
            _resident((None, 1, QKV_WIDTH), lambda i, bb: (layer, 0, 0)),
            pl.BlockSpec((8, tm, LANES), lambda i, bb: (0, i, 0)),
            _resident((PAIR, PAIR), lambda i, bb: (0, 0)),
        ],
        out_specs=pl.BlockSpec((None, tm, QKV_WIDTH), lambda i, bb: (bb, i, 0)),
        out_shape=jax.ShapeDtypeStruct((b, t, QKV_WIDTH), BF16),
        scratch_shapes=[pltpu.VMEM((tm, d), BF16)],
        compiler_params=_cparams(2),
        name="qkv_proj",
    )(x, nrm, w_in, gains, rope, ones)


def _stack_heads(q):
    q32 = q.astype(F32)
    lane = lax.broadcasted_iota(jnp.int32, q32.shape, 1)
    lo = lane < HEAD_DIM
    q0 = jnp.where(lo, q32, 0.0)
    q1 = jnp.where(lo, pltpu.roll(q32, HEAD_DIM, 1), 0.0)
    return jnp.concatenate([q0, q1], axis=0).astype(BF16)


def _scores(a, b):
    return lax.dot_general(a, b, (((1,), (1,)), ((), ())), preferred_element_type=F32)


def _ones_for_k(kv32):
    lane = lax.broadcasted_iota(jnp.int32, kv32.shape, 1)
    return jnp.where(lane < HEAD_DIM, 1.0, kv32).astype(BF16)


def _split_pv(o2):
    n = o2.shape[0] // 2
    lo = lax.broadcasted_iota(jnp.int32, (n, LANES), 1) < HEAD_DIM
    pv = jnp.where(lo, pltpu.roll(o2[:n], HEAD_DIM, 1), o2[n:])
    den = jnp.where(lo, o2[:n], pltpu.roll(o2[n:], HEAD_DIM, 1))
    return pv, den


def _attn_a_kernel(q_ref, kv_ref, o_ref, kv1_ref, s_ref, *, tk):
    t = kv_ref.shape[0]
    w = 2 * q_ref.shape[0]

    @pl.when(pl.program_id(2) == 0)
    def _():
        kv1_ref[...] = _ones_for_k(kv_ref[...].astype(F32))

    q2 = _stack_heads(q_ref[...])

    run_max = jnp.full((w, LANES), -jnp.inf, F32)
    for j in range(t // tk):
        sc = _scores(q2, kv_ref[j * tk:(j + 1) * tk, :])
        s_ref[:, j * tk:(j + 1) * tk] = sc
        for c in range(tk // LANES):
            run_max = jnp.maximum(run_max, sc[:, c * LANES:(c + 1) * LANES])
    m = jnp.max(run_max, axis=-1, keepdims=True)

    acc = jnp.zeros((w, LANES), F32)
    for j in range(t // tk):
        p = jnp.exp2(s_ref[:, j * tk:(j + 1) * tk] - m).astype(BF16)
        acc = acc + jnp.dot(p, kv1_ref[j * tk:(j + 1) * tk, :], preferred_element_type=F32)
    pv, den = _split_pv(acc)
    o_ref[...] = pv / den


def _attn_a(qkv, tq, tk):
    b, t, _ = qkv.shape
    return pl.pallas_call(
        functools.partial(_attn_a_kernel, tk=tk),
        grid=(b, A_GROUPS, t // tq),
        in_specs=[
            pl.BlockSpec((None, tq, LANES), lambda bb, g, i: (bb, i, Q_BLK["a"] + g)),
            pl.BlockSpec((None, t, LANES), lambda bb, g, i: (bb, 0, KV_BLK["a"] + g)),
        ],
        out_specs=pl.BlockSpec((None, tq, LANES), lambda bb, g, i: (bb, i, g)),
        out_shape=jax.ShapeDtypeStruct((b, t, A_GROUPS * LANES), F32),
        scratch_shapes=[pltpu.VMEM((t, LANES), BF16), pltpu.VMEM((2 * tq, t), F32)],
        compiler_params=_cparams(3),
        name="attn_global",
    )(qkv, qkv)


def _attn_b_kernel(q_ref, kv_ref, bias_ref, o_ref, kv1_ref, *, unroll):
    rows = q_ref.shape[0] // GRID_W
    n_keys = NA_ROWS * GRID_W
    kv1_ref[...] = _ones_for_k(kv_ref[...].astype(F32))

    def body(it, carry):
        offs, scores = [], []
        for u in range(unroll):
            r = it * unroll + u
            r0 = jnp.clip(r - NA_ROWS // 2, 0, rows - NA_ROWS)
            qoff = pl.multiple_of(r * GRID_W, GRID_W)
            koff = pl.multiple_of(r0 * GRID_W, GRID_W)
            q2 = _stack_heads(q_ref[pl.ds(qoff, GRID_W), :])
            scores.append(_scores(q2, kv_ref[pl.ds(koff, n_keys), :])
                          + bias_ref[r0 - r + (NA_ROWS - 1)])
            offs.append((qoff, koff))
        probs = [jnp.exp2(sc - jnp.max(sc, axis=-1, keepdims=True)).astype(BF16) for sc in scores]
        outs = [jnp.dot(p, kv1_ref[pl.ds(koff, n_keys), :], preferred_element_type=F32)
                for p, (_, koff) in zip(probs, offs)]
        for o2, (qoff, _) in zip(outs, offs):
            pv, den = _split_pv(o2)
            o_ref[pl.ds(qoff, GRID_W), :] = pv / den
        return carry

    lax.fori_loop(0, rows // unroll, body, 0)


def _attn_b(qkv, bias, layer, unroll):
    b, t, _ = qkv.shape
    return pl.pallas_call(
        functools.partial(_attn_b_kernel, unroll=unroll),
        grid=(B_GROUPS, b),
        in_specs=[
            pl.BlockSpec((None, t, LANES), lambda g, bb: (bb, 0, Q_BLK["b"] + g)),
            pl.BlockSpec((None, t, LANES), lambda g, bb: (bb, 0, KV_BLK["b"] + g)),
            pl.BlockSpec((None, None, NA_ROWS, 2 * GRID_W, NA_ROWS * GRID_W),
                         lambda g, bb: (layer, g, 0, 0, 0)),
        ],
        out_specs=pl.BlockSpec((None, t, LANES), lambda g, bb: (bb, 0, g)),
        out_shape=jax.ShapeDtypeStruct((b, t, B_GROUPS * LANES), F32),
        scratch_shapes=[pltpu.VMEM((t, LANES), BF16)],
        compiler_params=_cparams(2),
        name="attn_neighbourhood",
    )(qkv, qkv, bias)


def _attn_c_kernel(q_ref, kv_ref, bias_ref, o_ref, qf_ref, kvf_ref, qs_ref, kvs_ref, kv1_ref,
                   m_ref, w_ref, n_ref, *, tq, unroll):
    t = q_ref.shape[0]
    lo = lax.broadcasted_iota(jnp.int32, (tq, LANES), 1) < HEAD_DIM
    qf_ref[...] = q_ref[...].astype(F32)
    kvf_ref[...] = kv_ref[...].astype(F32)

    for branch, dil in enumerate(DILATIONS):
        length = t // dil
        n_tiles = length // tq
        n_keys = min(length, tq + 2 * HALF_WINDOW)
        first, last = branch == 0, branch == len(DILATIONS) - 1

        if dil == 1:
            q_src, kv_src = q_ref, kv_ref
            kv1_ref[...] = _ones_for_k(kvf_ref[...])
        else:
            q_src, kv_src = qs_ref, kvs_ref

            def gather_class(r, carry, dil=dil, length=length):
                rows = pl.ds(pl.multiple_of(r * length, LANES), length)
                qs_ref[rows, :] = qf_ref[pl.ds(r, length, stride=dil), :].astype(BF16)
                kvd = kvf_ref[pl.ds(r, length, stride=dil), :]
                kvs_ref[rows, :] = kvd.astype(BF16)
                kv1_ref[rows, :] = _ones_for_k(kvd)
                return carry

            lax.fori_loop(0, dil, gather_class, 0)

        def body(it, carry, dil=dil, length=length, n_tiles=n_tiles, n_keys=n_keys,
                 first=first, last=last, q_src=q_src, kv_src=kv_src):
            where, scores = [], []
            for u in range(unroll):
                g = it * unroll + u
                r = g // n_tiles
                i = g - r * n_tiles
                a0 = i * tq
                start = jnp.clip(a0 - HALF_WINDOW, 0, length - n_keys)
                kind = jnp.where(i == 0, 0, jnp.where(i == n_tiles - 1, 2, 1))
                qrow = pl.multiple_of(g * tq, tq)
                krow = pl.multiple_of(r * length + start, HALF_WINDOW)
                q2 = _stack_heads(q_src[pl.ds(qrow, tq), :])
                scores.append(_scores(q2, kv_src[pl.ds(krow, n_keys), :])
                              + bias_ref[kind, :, :n_keys])
                tok = pl.ds(qrow, tq) if dil == 1 else pl.ds(dil * a0 + r, tq, stride=dil)
                where.append((krow, tok))
            tops = [jnp.max(sc, axis=-1, keepdims=True) for sc in scores]
            probs = [jnp.exp2(sc - m).astype(BF16) for sc, m in zip(scores, tops)]
            outs = [jnp.dot(p, kv1_ref[pl.ds(krow, n_keys), :], preferred_element_type=F32)
                    for p, (krow, _) in zip(probs, where)]
            for o2, m, (_, tok) in zip(outs, tops, where):
                pv, den = _split_pv(o2)
                top = jnp.where(lo, jnp.broadcast_to(m[:tq], (tq, LANES)),
                                jnp.broadcast_to(m[tq:], (tq, LANES)))
                if first:
                    m_ref[tok, :] = top
                    w_ref[tok, :] = den
                    n_ref[tok, :] = pv
                else:
                    m_old = m_ref[tok, :]
                    m_new = jnp.maximum(m_old, top)
                    e_old = jnp.exp2(m_old - m_new)
                    e_new = jnp.exp2(top - m_new)
                    num = e_old * n_ref[tok, :] + e_new * pv
                    wsum = e_old * w_ref[tok, :] + e_new * den
                    if last:
                        o_ref[tok, :] = num / wsum
                    else:
                        m_ref[tok, :] = m_new
                        w_ref[tok, :] = wsum
                        n_ref[tok, :] = num
            return carry

        lax.fori_loop(0, (t // tq) // unroll, body, 0)


def _band_bias(tq):
    row = jnp.arange(2 * tq)[:, None] % tq
    col = jnp.arange(tq + 2 * HALF_WINDOW)[None, :]
    delta = jnp.arange(3)[:, None, None] * HALF_WINDOW
    return jnp.where(jnp.abs(delta + row - col) <= HALF_WINDOW, 0.0, NEG_INF).astype(F32)


def _attn_c(qkv, band, tq, unroll):
    b, t, _ = qkv.shape
    f32 = pltpu.VMEM((t, LANES), F32)
    b16 = pltpu.VMEM((t, LANES), BF16)
    return pl.pallas_call(
        functools.partial(_attn_c_kernel, tq=tq, unroll=unroll),
        grid=(C_GROUPS, b),
        in_specs=[
            pl.BlockSpec((None, t, LANES), lambda g, bb: (bb, 0, Q_BLK["c"] + g)),
            pl.BlockSpec((None, t, LANES), lambda g, bb: (bb, 0, KV_BLK["c"] + g)),
            _resident(band.shape, lambda g, bb: (0, 0, 0)),
        ],
        out_specs=pl.BlockSpec((None, t, LANES), lambda g, bb: (bb, 0, g)),
        out_shape=jax.ShapeDtypeStruct((b, t, C_GROUPS * LANES), F32),
        scratch_shapes=[f32, f32, b16, b16, b16, f32, f32, f32],
        compiler_params=_cparams(2),
        name="attn_dilated",
    )(qkv, qkv, band)


def _tail_kernel(x_ref, oa_ref, ob_ref, oc_ref, og_ref, wo_ref, nf_ref, wgu_ref, wd_ref,
                 out_ref, y_ref, act_ref, *, fchunk):
    a_w = A_GROUPS * LANES
    b_w = B_GROUPS * LANES
    y_ref[:, :a_w] = _rms(oa_ref[...], og_ref[:, :a_w]).astype(BF16)
    y_ref[:, a_w:a_w + b_w] = _rms(ob_ref[...], og_ref[:, a_w:a_w + b_w]).astype(BF16)
    y_ref[:, a_w + b_w:] = _rms(oc_ref[...], og_ref[:, a_w + b_w:]).astype(BF16)

    x1 = x_ref[...] + jnp.dot(y_ref[...], wo_ref[...], preferred_element_type=F32)
    h = _rms(x1, nf_ref[...]).astype(BF16)
    for c in range(FFN_DIM // fchunk):
        gate = jnp.dot(h, wgu_ref[:, c * fchunk:(c + 1) * fchunk], preferred_element_type=F32)
        up = jnp.dot(h, wgu_ref[:, FFN_DIM + c * fchunk:FFN_DIM + (c + 1) * fchunk],
                     preferred_element_type=F32)
        act_ref[:, c * fchunk:(c + 1) * fchunk] = (jax.nn.silu(gate) * up).astype(BF16)
    out_ref[...] = x1 + jnp.dot(act_ref[...], wd_ref[...], preferred_element_type=F32)


def _tail(x, oa, ob, oc, layer, out_gain, w_out, norm_ffn, w_gu, w_down, tm, fchunk):
    b, t, d = x.shape
    n = b * t
    flat = lambda a: a.reshape(n, a.shape[-1])
    row = lambda w: pl.BlockSpec((tm, w), lambda i: (i, 0))
    out = pl.pallas_call(
        functools.partial(_tail_kernel, fchunk=fchunk),
        grid=(n // tm,),
        in_specs=[
            row(d), row(A_GROUPS * LANES), row(B_GROUPS * LANES), row(C_GROUPS * LANES),
            _resident((None, 1, d), lambda i: (layer, 0, 0)),
            _resident((None, d, d), lambda i: (layer, 0, 0)),
            _resident((None, 1, d), lambda i: (layer, 0, 0)),
            _resident((None, d, 2 * FFN_DIM), lambda i: (layer, 0, 0)),
            _resident((None, FFN_DIM, d), lambda i: (layer, 0, 0)),
        ],
        out_specs=row(d),
        out_shape=jax.ShapeDtypeStruct((n, d), F32),
        scratch_shapes=[pltpu.VMEM((tm, d), BF16), pltpu.VMEM((tm, FFN_DIM), BF16)],
        compiler_params=_cparams(1),
        name="out_proj_ffn",
    )(flat(x), flat(oa), flat(ob), flat(oc),
      out_gain, w_out, norm_ffn, w_gu, w_down)
    return out.reshape(b, t, d)


def _rope_tables(pos, dim):
    inv_freq = ROPE_THETA ** (-jnp.arange(0, dim, 2, dtype=F32) / dim)
    ang = pos[:, None] * inv_freq[None, :]
    ang = jnp.concatenate([ang, ang], axis=-1)
    return jnp.cos(ang), jnp.sin(ang)


def _rope_stack(t):
    pos = jnp.arange(t, dtype=jnp.int32)
    half = HEAD_DIM // 2
    cos_r, sin_r = _rope_tables((pos // GRID_W).astype(F32), half)
    cos_c, sin_c = _rope_tables((pos % GRID_W).astype(F32), half)
    cos_1, sin_1 = _rope_tables(pos.astype(F32), HEAD_DIM)
    j = jnp.arange(HEAD_DIM)
    cos_a = jnp.concatenate([cos_r, cos_c], axis=-1)
    sin_a = jnp.concatenate([sin_r, sin_c], axis=-1) * jnp.where(j % half < half // 2, -1.0, 1.0)
    sin_1 = sin_1 * jnp.where(j < half, -1.0, 1.0)
    one, zero = jnp.ones_like(cos_a), jnp.zeros_like(cos_a)
    cat = lambda u, v: jnp.concatenate([u, v], axis=-1)
    return jnp.stack([
        cat(cos_a, cos_a) * SCALE, cat(sin_a, sin_a) * SCALE, cat(cos_a, one), cat(sin_a, zero),
        cat(cos_1, cos_1) * SCALE, cat(sin_1, sin_1) * SCALE, cat(cos_1, one), cat(sin_1, zero),
    ]).astype(F32)


def _neighbourhood_bias(rpb):
    depth = rpb.shape[0]
    qc = jnp.arange(GRID_W)[:, None]
    kc = jnp.arange(GRID_W)[None, :]
    c0 = jnp.clip(qc - NA_COLS // 2, 0, GRID_W - NA_COLS)
    valid = (kc >= c0) & (kc < c0 + NA_COLS)
    dc = jnp.clip(kc - qc + NA_COLS - 1, 0, 2 * NA_COLS - 2)
    dr = jnp.arange(NA_ROWS)[:, None] + jnp.arange(NA_ROWS)[None, :]
    tab = rpb[:, :, dr][:, :, :, :, dc]
    tab = jnp.where(valid[None, None, None, None], tab * LOG2E, NEG_INF)
    tab = jnp.transpose(tab, (0, 1, 2, 4, 3, 5))
    tab = tab.reshape(depth, B_GROUPS, 2, NA_ROWS, GRID_W, NA_ROWS * GRID_W)
    tab = jnp.transpose(tab, (0, 1, 3, 2, 4, 5))
    return tab.reshape(depth, B_GROUPS, NA_ROWS, 2 * GRID_W, NA_ROWS * GRID_W).astype(F32)


def _prepare(norm_mix, w_in, q_gain, k_gain, rpb, out_gain, w_out, norm_ffn, w_gate_up, w_down):
    depth = w_in.shape[0]
    q_w = N_HEADS * HEAD_DIM
    kv_w = KV_HEADS * HEAD_DIM
    wq = w_in[..., :q_w]
    wk = w_in[..., q_w:q_w + kv_w].reshape(depth, D_MODEL, KV_HEADS, 1, HEAD_DIM)
    wv = w_in[..., q_w + kv_w:].reshape(depth, D_MODEL, KV_HEADS, 1, HEAD_DIM)
    wkv = jnp.concatenate([wk, wv], axis=3).reshape(depth, D_MODEL, 2 * kv_w)
    w_qkv = jnp.concatenate([wq, wkv], axis=-1).astype(BF16)

    heads_q = (4, 6, 6)
    heads_kv = (2, 3, 3)
    gq = jnp.concatenate([jnp.tile(q_gain[:, m], (1, heads_q[m])) for m in range(3)], axis=-1)
    ones = jnp.ones((depth, HEAD_DIM), F32)
    gkv = jnp.concatenate(
        [jnp.tile(jnp.concatenate([k_gain[:, m], ones], axis=-1), (1, heads_kv[m]))
         for m in range(3)], axis=-1)
    gains = jnp.concatenate([gq, gkv], axis=-1).reshape(depth, 1, QKV_WIDTH).astype(F32)

    seg = jnp.arange(PAIR) // HEAD_DIM
    head_ones = (seg[:, None] == seg[None, :]).astype(BF16)
    return dict(
        norm_mix=norm_mix.reshape(depth, 1, D_MODEL), w_qkv=w_qkv, gains=gains,
        head_ones=head_ones, out_gain=out_gain.reshape(depth, 1, D_MODEL),
        w_out=w_out.astype(BF16), norm_ffn=norm_ffn.reshape(depth, 1, D_MODEL),
        w_gu=w_gate_up.astype(BF16), w_down=w_down.astype(BF16),
        bias=_neighbourhood_bias(rpb), band=_band_bias(TQ_C))


def _trunk(x, prm, *, tm_qkv=512, tq_a=128, tk_a=512, unroll_b=16, unroll_c=8, tm_tail=256,
           fchunk=256):
    b, t, _ = x.shape
    depth = prm["w_qkv"].shape[0]
    rope = _rope_stack(t)
    bias, band, tq_c = prm["bias"], prm["band"], TQ_C
    for layer in range(depth):
        qkv = _qkv_proj(x, layer, prm["norm_mix"], prm["w_qkv"], prm["gains"], rope,
                        prm["head_ones"], tm_qkv)
        oa = _attn_a(qkv, tq_a, tk_a)
        ob = _attn_b(qkv, bias, layer, unroll_b)
        oc = _attn_c(qkv, band, tq_c, unroll_c)
        x = _tail(x, oa, ob, oc, layer, prm["out_gain"], prm["w_out"], prm["norm_ffn"],
                  prm["w_gu"], prm["w_down"], tm_tail, fchunk)
    return x


def kernel(x_prompt, x_sample, norm_mix, w_in, q_gain, k_gain, rpb, out_gain, w_out, norm_ffn,
           w_gate_up, w_down):
    prm = _prepare(norm_mix, w_in, q_gain, k_gain, rpb, out_gain, w_out, norm_ffn, w_gate_up,
                   w_down)
    return (_trunk(x_prompt, prm), _trunk(x_sample, prm))
```

```python
import functools

import jax
import jax.numpy as jnp
from jax import lax
from jax.experimental import pallas as pl
from jax.experimental.pallas import tpu as pltpu

F32 = jnp.float32
BF16 = jnp.bfloat16

D_MODEL = 1024
HEAD_DIM = 64
N_HEADS = 16
KV_HEADS = 8
QKV_WIDTH = 2048
FFN_DIM = 2816
GRID_W = 64
NA_ROWS = 8
NA_COLS = 16
STEP = 4
DILATIONS = (1, STEP, STEP * STEP)
HALF_WINDOW = 64
TQ_C = 2 * HALF_WINDOW
ROPE_THETA = 10000.0
RMS_EPS = 1e-6
NEG_INF = -1e30
LOG2E = 1.4426950408889634
SCALE = HEAD_DIM ** -0.5 * LOG2E

LANES = 128
PAIR = 2 * LANES
A_GROUPS, B_GROUPS, C_GROUPS = 2, 3, 3
Q_BLK = {"a": 0, "b": 2, "c": 5}
KV_BLK = {"a": 8, "b": 10, "c": 13}
N_BLKS = QKV_WIDTH // LANES

VMEM_LIMIT = 56 * 1024 * 1024


def _cparams(n_axes):
    return pltpu.CompilerParams(
        dimension_semantics=("arbitrary",) * n_axes, vmem_limit_bytes=VMEM_LIMIT)


def _resident(block_shape, index_map):
    return pl.BlockSpec(block_shape, index_map, pipeline_mode=pl.Buffered(1))


def _rms(x, gain):
    ms = jnp.mean(x * x, axis=-1, keepdims=True)
    return x * lax.rsqrt(ms + RMS_EPS) * gain


def _rotate(y, lane, half):
    up = pltpu.roll(y, LANES - half, 1)
    dn = pltpu.roll(y, half, 1)
    return jnp.where((lane % (2 * half)) < half, up, dn)


def _qkv_kernel(x_ref, nrm_ref, w_ref, gain_ref, rope_ref, ones_ref, out_ref, h_ref, *, chunk):
    tm = x_ref.shape[0]
    h_ref[...] = _rms(x_ref[...], nrm_ref[...]).astype(BF16)
    lane = lax.broadcasted_iota(jnp.int32, (tm, LANES), 1)
    is_k = lane < HEAD_DIM
    for c in range(QKV_WIDTH // chunk):
        wide = jnp.dot(h_ref[...], w_ref[:, c * chunk:(c + 1) * chunk], preferred_element_type=F32)
        for hb in range(chunk // LANES):
            if hb % 2 == 0:
                p = wide[:, hb * LANES:hb * LANES + PAIR]
                ss = jnp.dot((p * p).astype(BF16), ones_ref[...], preferred_element_type=F32)
                inv = lax.rsqrt(ss * (1.0 / HEAD_DIM) + RMS_EPS)
            blk = c * (chunk // LANES) + hb
            cols = slice((hb % 2) * LANES, (hb % 2 + 1) * LANES)
            is_q = blk < KV_BLK["a"]
            scale = inv[:, cols] if is_q else jnp.where(is_k, inv[:, cols], 1.0)
            y = p[:, cols] * scale * gain_ref[:, blk * LANES:(blk + 1) * LANES]
            if blk < Q_BLK["b"]:
                tab, half = 0, HEAD_DIM // 4
            elif blk < Q_BLK["c"]:
                tab, half = None, None
            elif is_q:
                tab, half = 4, HEAD_DIM // 2
            elif blk < KV_BLK["b"]:
                tab, half = 2, HEAD_DIM // 4
            elif blk < KV_BLK["c"]:
                tab, half = None, None
            else:
                tab, half = 6, HEAD_DIM // 2
            if tab is not None:
                y = y * rope_ref[tab] + _rotate(y, lane, half) * rope_ref[tab + 1]
            elif is_q:
                y = y * SCALE
            out_ref[:, blk * LANES:(blk + 1) * LANES] = y.astype(BF16)


def _qkv_proj(x, layer, nrm, w_in, gains, rope, ones, tm):
    b, t, d = x.shape
    return pl.pallas_call(
        functools.partial(_qkv_kernel, chunk=4 * LANES),
        grid=(t // tm, b),
        in_specs=[
            pl.BlockSpec((None, tm, d), lambda i, bb: (bb, i, 0)),
            _resident((None, 1, d), lambda i, bb: (layer, 0, 0)),
            _resident((None, d, QKV_WIDTH), lambda i, bb: (layer, 0, 0)),
            _resident((None, 1, QKV_WIDTH), lambda i, bb: (layer, 0, 0)),
            pl.BlockSpec((8, tm, LANES), lambda i, bb: (0, i, 0)),
            _resident((PAIR, PAIR), lambda i, bb: (0, 0)),
        ],
        out_specs=pl.BlockSpec((None, tm, QKV_WIDTH), lambda i, bb: (bb, i, 0)),
        out_shape=jax.ShapeDtypeStruct((b, t, QKV_WIDTH), BF16),
        scratch_shapes=[pltpu.VMEM((tm, d), BF16)],
        compiler_params=_cparams(2),
        name="qkv_proj",
    )(x, nrm, w_in, gains, rope, ones)


def _stack_heads(q):
    q32 = q.astype(F32)
    lane = lax.broadcasted_iota(jnp.int32, q32.shape, 1)
    lo = lane < HEAD_DIM
    q0 = jnp.where(lo, q32, 0.0)
    q1 = jnp.where(lo, pltpu.roll(q32, HEAD_DIM, 1), 0.0)
    return jnp.concatenate([q0, q1], axis=0).astype(BF16)


def _scores(a, b):
    return lax.dot_general(a, b, (((1,), (1,)), ((), ())), preferred_element_type=F32)


def _ones_for_k(kv32):
    lane = lax.broadcasted_iota(jnp.int32, kv32.shape, 1)
    return jnp.where(lane < HEAD_DIM, 1.0, kv32).astype(BF16)


def _split_pv(o2):
    n = o2.shape[0] // 2
    lo = lax.broadcasted_iota(jnp.int32, (n, LANES), 1) < HEAD_DIM
    pv = jnp.where(lo, pltpu.roll(o2[:n], HEAD_DIM, 1), o2[n:])
    den = jnp.where(lo, o2[:n], pltpu.roll(o2[n:], HEAD_DIM, 1))
    return pv, den


def _attn_a_kernel(q_ref, kv_ref, o_ref, kv1_ref, s_ref, *, tk, sub):
    t = kv_ref.shape[0]
    n_sub = q_ref.shape[0] // sub
    w = 2 * sub

    @pl.when(pl.program_id(2) == 0)
    def _():
        kv1_ref[...] = _ones_for_k(kv_ref[...].astype(F32))

    tops = []
    for h in range(n_sub):
        q2 = _stack_heads(q_ref[h * sub:(h + 1) * sub, :])
        run_max = jnp.full((w, LANES), -jnp.inf, F32)
        for j in range(t // tk):
            sc = _scores(q2, kv_ref[j * tk:(j + 1) * tk, :])
            s_ref[h, :, j * tk:(j + 1) * tk] = sc
            for c in range(tk // LANES):
                run_max = jnp.maximum(run_max, sc[:, c * LANES:(c + 1) * LANES])
        tops.append(jnp.max(run_max, axis=-1, keepdims=True))

    for h in range(n_sub):
        acc = jnp.zeros((w, LANES), F32)
        for j in range(t // tk):
            p = jnp.exp2(s_ref[h, :, j * tk:(j + 1) * tk] - tops[h]).astype(BF16)
            acc = acc + jnp.dot(p, kv1_ref[j * tk:(j + 1) * tk, :], preferred_element_type=F32)
        pv, den = _split_pv(acc)
        o_ref[h * sub:(h + 1) * sub, :] = pv / den


def _attn_a(qkv, tq, tk, sub):
    b, t, _ = qkv.shape
    return pl.pallas_call(
        functools.partial(_attn_a_kernel, tk=tk, sub=sub),
        grid=(b, A_GROUPS, t // tq),
        in_specs=[
            pl.BlockSpec((None, tq, LANES), lambda bb, g, i: (bb, i, Q_BLK["a"] + g)),
            pl.BlockSpec((None, t, LANES), lambda bb, g, i: (bb, 0, KV_BLK["a"] + g)),
        ],
        out_specs=pl.BlockSpec((None, tq, LANES), lambda bb, g, i: (bb, i, g)),
        out_shape=jax.ShapeDtypeStruct((b, t, A_GROUPS * LANES), F32),
        scratch_shapes=[pltpu.VMEM((t, LANES), BF16), pltpu.VMEM((tq // sub, 2 * sub, t), F32)],
        compiler_params=_cparams(3),
        name="attn_global",
    )(qkv, qkv)


def _attn_b_kernel(q_ref, kv_ref, bias_ref, o_ref, kv1_ref, *, unroll):
    rows = q_ref.shape[0] // GRID_W
    n_keys = NA_ROWS * GRID_W
    kv1_ref[...] = _ones_for_k(kv_ref[...].astype(F32))

    def body(it, carry):
        offs, scores = [], []
        for u in range(unroll):
            r = it * unroll + u
            r0 = jnp.clip(r - NA_ROWS // 2, 0, rows - NA_ROWS)
            qoff = pl.multiple_of(r * GRID_W, GRID_W)
            koff = pl.multiple_of(r0 * GRID_W, GRID_W)
            q2 = _stack_heads(q_ref[pl.ds(qoff, GRID_W), :])
            scores.append(_scores(q2, kv_ref[pl.ds(koff, n_keys), :])
                          + bias_ref[r0 - r + (NA_ROWS - 1)])
            offs.append((qoff, koff))
        probs = [jnp.exp2(sc - jnp.max(sc, axis=-1, keepdims=True)).astype(BF16) for sc in scores]
        outs = [jnp.dot(p, kv1_ref[pl.ds(koff, n_keys), :], preferred_element_type=F32)
                for p, (_, koff) in zip(probs, offs)]
        for o2, (qoff, _) in zip(outs, offs):
            pv, den = _split_pv(o2)
            o_ref[pl.ds(qoff, GRID_W), :] = pv / den
        return carry

    lax.fori_loop(0, rows // unroll, body, 0)


def _attn_b(qkv, bias, layer, unroll):
    b, t, _ = qkv.shape
    return pl.pallas_call(
        functools.partial(_attn_b_kernel, unroll=unroll),
        grid=(B_GROUPS, b),
        in_specs=[
            pl.BlockSpec((None, t, LANES), lambda g, bb: (bb, 0, Q_BLK["b"] + g)),
            pl.BlockSpec((None, t, LANES), lambda g, bb: (bb, 0, KV_BLK["b"] + g)),
            pl.BlockSpec((None, None, NA_ROWS, 2 * GRID_W, NA_ROWS * GRID_W),
                         lambda g, bb: (layer, g, 0, 0, 0)),
        ],
        out_specs=pl.BlockSpec((None, t, LANES), lambda g, bb: (bb, 0, g)),
        out_shape=jax.ShapeDtypeStruct((b, t, B_GROUPS * LANES), F32),
        scratch_shapes=[pltpu.VMEM((t, LANES), BF16)],
        compiler_params=_cparams(2),
        name="attn_neighbourhood",
    )(qkv, qkv, bias)


def _natural_residue(cls, level):
    res = 0
    for d in range(level):
        res = res + ((cls // (STEP ** (level - 1 - d))) % STEP) * (STEP ** d)
    return res


def _attn_c_kernel(q_ref, kv_ref, bias_ref, o_ref, qf_ref, kvf_ref, qg_ref, kvg_ref, qs_ref,
                   kvs_ref, kv1_ref, ma_ref, wa_ref, na_ref, mb_ref, wb_ref, nb_ref, *, tq,
                   unroll):
    t = q_ref.shape[0]
    lo = lax.broadcasted_iota(jnp.int32, (tq, LANES), 1) < HEAD_DIM
    qf_ref[...] = q_ref[...].astype(F32)
    kvf_ref[...] = kv_ref[...].astype(F32)
    f32_src = (qf_ref, kvf_ref)
    states = ((ma_ref, wa_ref, na_ref), (mb_ref, wb_ref, nb_ref))

    for level, dil in enumerate(DILATIONS):
        length = t // dil
        n_tiles = length // tq
        n_keys = min(length, tq + 2 * HALF_WINDOW)
        first, last = level == 0, level == len(DILATIONS) - 1
        prev_len = length * STEP
        st_in, st_out = states[(level + 1) % 2], states[level % 2]

        if first:
            q_src, kv_src = q_ref, kv_ref
            kv1_ref[...] = _ones_for_k(kvf_ref[...])
        else:
            q_src, kv_src = qs_ref, kvs_ref
            f32_dst = None if last else (qg_ref, kvg_ref)

            def gather_class(c, carry, length=length, prev_len=prev_len, f32_src=f32_src,
                             f32_dst=f32_dst):
                parent = c // STEP
                src = pl.ds(parent * prev_len + (c - parent * STEP), length, stride=STEP)
                rows = pl.ds(pl.multiple_of(c * length, LANES), length)
                qd = f32_src[0][src, :]
                kvd = f32_src[1][src, :]
                qs_ref[rows, :] = qd.astype(BF16)
                kvs_ref[rows, :] = kvd.astype(BF16)
                kv1_ref[rows, :] = _ones_for_k(kvd)
                if f32_dst is not None:
                    f32_dst[0][rows, :] = qd
                    f32_dst[1][rows, :] = kvd
                return carry

            lax.fori_loop(0, dil, gather_class, 0)
            f32_src = f32_dst

        def body(it, carry, level=level, dil=dil, length=length, n_tiles=n_tiles, n_keys=n_keys,
                 first=first, last=last, prev_len=prev_len, q_src=q_src, kv_src=kv_src,
                 st_in=st_in, st_out=st_out):
            where, scores = [], []
            for u in range(unroll):
                g = it * unroll + u
                c = g // n_tiles
                i = g - c * n_tiles
                a0 = i * tq
                start = jnp.clip(a0 - HALF_WINDOW, 0, length - n_keys)
                kind = jnp.where(i == 0, 0, jnp.where(i == n_tiles - 1, 2, 1))
                own = pl.ds(pl.multiple_of(g * tq, tq), tq)
                krow = pl.multiple_of(c * length + start, HALF_WINDOW)
                q2 = _stack_heads(q_src[own, :])
                scores.append(_scores(q2, kv_src[pl.ds(krow, n_keys), :])
                              + bias_ref[kind, :, :n_keys])
                parent = c // STEP
                prev = pl.ds(parent * prev_len + STEP * a0 + (c - parent * STEP), tq, stride=STEP)
                out = pl.ds(dil * a0 + _natural_residue(c, level), tq, stride=dil)
                where.append((krow, own, prev, out))
            tops = [jnp.max(sc, axis=-1, keepdims=True) for sc in scores]
            probs = [jnp.exp2(sc - m).astype(BF16) for sc, m in zip(scores, tops)]
            outs = [jnp.dot(p, kv1_ref[pl.ds(krow, n_keys), :], preferred_element_type=F32)
                    for p, (krow, _, _, _) in zip(probs, where)]
            for o2, m, (_, own, prev, out) in zip(outs, tops, where):
                pv, den = _split_pv(o2)
                top = jnp.where(lo, jnp.broadcast_to(m[:tq], (tq, LANES)),
                                jnp.broadcast_to(m[tq:], (tq, LANES)))
                if not first:
                    m_old = st_in[0][prev, :]
                    m_new = jnp.maximum(m_old, top)
                    e_old = jnp.exp2(m_old - m_new)
                    e_new = jnp.exp2(top - m_new)
                    pv = e_old * st_in[2][prev, :] + e_new * pv
                    den = e_old * st_in[1][prev, :] + e_new * den
                    top = m_new
                if last:
                    o_ref[out, :] = pv / den
                else:
                    st_out[0][own, :] = top
                    st_out[1][own, :] = den
                    st_out[2][own, :] = pv
            return carry

        lax.fori_loop(0, (t // tq) // unroll, body, 0)


def _band_bias(tq):
    row = jnp.arange(2 * tq)[:, None] % tq
    col = jnp.arange(tq + 2 * HALF_WINDOW)[None, :]
    delta = jnp.arange(3)[:, None, None] * HALF_WINDOW
    return jnp.where(jnp.abs(delta + row - col) <= HALF_WINDOW, 0.0, NEG_INF).astype(F32)


def _attn_c(qkv, band, tq, unroll):
    b, t, _ = qkv.shape
    f32 = pltpu.VMEM((t, LANES), F32)
    b16 = pltpu.VMEM((t, LANES), BF16)
    return pl.pallas_call(
        functools.partial(_attn_c_kernel, tq=tq, unroll=unroll),
        grid=(C_GROUPS, b),
        in_specs=[
            pl.BlockSpec((None, t, LANES), lambda g, bb: (bb, 0, Q_BLK["c"] + g)),
            pl.BlockSpec((None, t, LANES), lambda g, bb: (bb, 0, KV_BLK["c"] + g)),
            _resident(band.shape, lambda g, bb: (0, 0, 0)),
        ],
        out_specs=pl.BlockSpec((None, t, LANES), lambda g, bb: (bb, 0, g)),
        out_shape=jax.ShapeDtypeStruct((b, t, C_GROUPS * LANES), F32),
        scratch_shapes=[f32] * 4 + [b16] * 3 + [f32] * 6,
        compiler_params=_cparams(2),
        name="attn_dilated",
    )(qkv, qkv, band)


def _tail_kernel(x_ref, oa_ref, ob_ref, oc_ref, og_ref, wo_ref, nf_ref, wgu_ref, wd_ref,
                 out_ref, y_ref, act_ref, *, fchunk):
    a_w = A_GROUPS * LANES
    b_w = B_GROUPS * LANES
    y_ref[:, :a_w] = _rms(oa_ref[...], og_ref[:, :a_w]).astype(BF16)
    y_ref[:, a_w:a_w + b_w] = _rms(ob_ref[...], og_ref[:, a_w:a_w + b_w]).astype(BF16)
    y_ref[:, a_w + b_w:] = _rms(oc_ref[...], og_ref[:, a_w + b_w:]).astype(BF16)

    x1 = x_ref[...] + jnp.dot(y_ref[...], wo_ref[...], preferred_element_type=F32)
    h = _rms(x1, nf_ref[...]).astype(BF16)
    for c in range(FFN_DIM // fchunk):
        gate = jnp.dot(h, wgu_ref[:, c * fchunk:(c + 1) * fchunk], preferred_element_type=F32)
        up = jnp.dot(h, wgu_ref[:, FFN_DIM + c * fchunk:FFN_DIM + (c + 1) * fchunk],
                     preferred_element_type=F32)
        act_ref[:, c * fchunk:(c + 1) * fchunk] = (jax.nn.silu(gate) * up).astype(BF16)
    out_ref[...] = x1 + jnp.dot(act_ref[...], wd_ref[...], preferred_element_type=F32)


def _tail(x, oa, ob, oc, layer, out_gain, w_out, norm_ffn, w_gu, w_down, tm, fchunk):
    b, t, d = x.shape
    n = b * t
    flat = lambda a: a.reshape(n, a.shape[-1])
    row = lambda w: pl.BlockSpec((tm, w), lambda i: (i, 0))
    out = pl.pallas_call(
        functools.partial(_tail_kernel, fchunk=fchunk),
        grid=(n // tm,),
        in_specs=[
            row(d), row(A_GROUPS * LANES), row(B_GROUPS * LANES), row(C_GROUPS * LANES),
            _resident((None, 1, d), lambda i: (layer, 0, 0)),
            _resident((None, d, d), lambda i: (layer, 0, 0)),
            _resident((None, 1, d), lambda i: (layer, 0, 0)),
            _resident((None, d, 2 * FFN_DIM), lambda i: (layer, 0, 0)),
            _resident((None, FFN_DIM, d), lambda i: (layer, 0, 0)),
        ],
        out_specs=row(d),
        out_shape=jax.ShapeDtypeStruct((n, d), F32),
        scratch_shapes=[pltpu.VMEM((tm, d), BF16), pltpu.VMEM((tm, FFN_DIM), BF16)],
        compiler_params=_cparams(1),
        name="out_proj_ffn",
    )(flat(x), flat(oa), flat(ob), flat(oc),
      out_gain, w_out, norm_ffn, w_gu, w_down)
    return out.reshape(b, t, d)


def _rope_tables(pos, dim):
    inv_freq = ROPE_THETA ** (-jnp.arange(0, dim, 2, dtype=F32) / dim)
    ang = pos[:, None] * inv_freq[None, :]
    ang = jnp.concatenate([ang, ang], axis=-1)
    return jnp.cos(ang), jnp.sin(ang)


def _rope_stack(t):
    pos = jnp.arange(t, dtype=jnp.int32)
    half = HEAD_DIM // 2
    cos_r, sin_r = _rope_tables((pos // GRID_W).astype(F32), half)
    cos_c, sin_c = _rope_tables((pos % GRID_W).astype(F32), half)
    cos_1, sin_1 = _rope_tables(pos.astype(F32), HEAD_DIM)
    j = jnp.arange(HEAD_DIM)
    cos_a = jnp.concatenate([cos_r, cos_c], axis=-1)
    sin_a = jnp.concatenate([sin_r, sin_c], axis=-1) * jnp.where(j % half < half // 2, -1.0, 1.0)
    sin_1 = sin_1 * jnp.where(j < half, -1.0, 1.0)
    one, zero = jnp.ones_like(cos_a), jnp.zeros_like(cos_a)
    cat = lambda u, v: jnp.concatenate([u, v], axis=-1)
    return jnp.stack([
        cat(cos_a, cos_a) * SCALE, cat(sin_a, sin_a) * SCALE, cat(cos_a, one), cat(sin_a, zero),
        cat(cos_1, cos_1) * SCALE, cat(sin_1, sin_1) * SCALE, cat(cos_1, one), cat(sin_1, zero),
    ]).astype(F32)


def _neighbourhood_bias(rpb):
    depth, _, n_dr, n_dc = rpb.shape
    row = jnp.arange(2 * GRID_W)
    lane = jnp.arange(NA_ROWS * GRID_W)
    head = jnp.arange(B_GROUPS)[:, None, None, None] * 2 + (row // GRID_W)[None, None, :, None]
    qc = (row % GRID_W)[:, None]
    kc = (lane % GRID_W)[None, :]
    c0 = jnp.clip(qc - NA_COLS // 2, 0, GRID_W - NA_COLS)
    valid = (kc >= c0) & (kc < c0 + NA_COLS)
    dc = jnp.clip(kc - qc + NA_COLS - 1, 0, n_dc - 1)
    dr = jnp.arange(NA_ROWS)[None, :, None, None] + (lane // GRID_W)[None, None, None, :]
    flat = (head * n_dr + dr) * n_dc + dc[None, None]
    tab = jnp.take((rpb * LOG2E).reshape(depth, -1), flat, axis=1)
    return jnp.where(valid[None, None, None], tab, NEG_INF).astype(F32)


def _prepare(norm_mix, w_in, q_gain, k_gain, rpb, out_gain, w_out, norm_ffn, w_gate_up, w_down):
    depth = w_in.shape[0]
    q_w = N_HEADS * HEAD_DIM
    kv_w = KV_HEADS * HEAD_DIM
    wq = w_in[..., :q_w]
    wk = w_in[..., q_w:q_w + kv_w].reshape(depth, D_MODEL, KV_HEADS, 1, HEAD_DIM)
    wv = w_in[..., q_w + kv_w:].reshape(depth, D_MODEL, KV_HEADS, 1, HEAD_DIM)
    wkv = jnp.concatenate([wk, wv], axis=3).reshape(depth, D_MODEL, 2 * kv_w)
    w_qkv = jnp.concatenate([wq, wkv], axis=-1).astype(BF16)

    heads_q = (4, 6, 6)
    heads_kv = (2, 3, 3)
    gq = jnp.concatenate([jnp.tile(q_gain[:, m], (1, heads_q[m])) for m in range(3)], axis=-1)
    ones = jnp.ones((depth, HEAD_DIM), F32)
    gkv = jnp.concatenate(
        [jnp.tile(jnp.concatenate([k_gain[:, m], ones], axis=-1), (1, heads_kv[m]))
         for m in range(3)], axis=-1)
    gains = jnp.concatenate([gq, gkv], axis=-1).reshape(depth, 1, QKV_WIDTH).astype(F32)

    seg = jnp.arange(PAIR) // HEAD_DIM
    head_ones = (seg[:, None] == seg[None, :]).astype(BF16)
    return dict(
        norm_mix=norm_mix.reshape(depth, 1, D_MODEL), w_qkv=w_qkv, gains=gains,
        head_ones=head_ones, out_gain=out_gain.reshape(depth, 1, D_MODEL),
        w_out=w_out.astype(BF16), norm_ffn=norm_ffn.reshape(depth, 1, D_MODEL),
        w_gu=w_gate_up.astype(BF16), w_down=w_down.astype(BF16),
        bias=_neighbourhood_bias(rpb), band=_band_bias(TQ_C))


def _trunk(x, prm, *, tm_qkv=512, tq_a=512, tk_a=512, sub_a=128, unroll_b=16, unroll_c=8,
           tm_tail=256, fchunk=256):
    b, t, _ = x.shape
    depth = prm["w_qkv"].shape[0]
    rope = _rope_stack(t)
    bias, band, tq_c = prm["bias"], prm["band"], TQ_C
    for layer in range(depth):
        qkv = _qkv_proj(x, layer, prm["norm_mix"], prm["w_qkv"], prm["gains"], rope,
                        prm["head_ones"], tm_qkv)
        oa = _attn_a(qkv, tq_a, tk_a, sub_a)
        ob = _attn_b(qkv, bias, layer, unroll_b)
        oc = _attn_c(qkv, band, tq_c, unroll_c)
        x = _tail(x, oa, ob, oc, layer, prm["out_gain"], prm["w_out"], prm["norm_ffn"],
                  prm["w_gu"], prm["w_down"], tm_tail, fchunk)
    return x


def kernel(x_prompt, x_sample, norm_mix, w_in, q_gain, k_gain, rpb, out_gain, w_out, norm_ffn,
           w_gate_up, w_down):
    prm = _prepare(norm_mix, w_in, q_gain, k_gain, rpb, out_gain, w_out, norm_ffn, w_gate_up,
                   w_down)
    return (_trunk(x_prompt, prm), _trunk(x_sample, prm))
```

```python
import functools

import jax
import jax.numpy as jnp
from jax import lax
from jax.experimental import pallas as pl
from jax.experimental.pallas import tpu as pltpu

F32 = jnp.float32
BF16 = jnp.bfloat16

D_MODEL = 1024
HEAD_DIM = 64
N_HEADS = 16
KV_HEADS = 8
QKV_WIDTH = 2048
FFN_DIM = 2816
GRID_W = 64
NA_ROWS = 8
NA_COLS = 16
STEP = 4
DILATIONS = (1, STEP, STEP * STEP)
HALF_WINDOW = 64
TQ_C = 2 * HALF_WINDOW
ROPE_THETA = 10000.0
RMS_EPS = 1e-6
NEG_INF = -1e30
LOG2E = 1.4426950408889634
SCALE = HEAD_DIM ** -0.5 * LOG2E

LANES = 128
PAIR = 2 * LANES
A_GROUPS, B_GROUPS, C_GROUPS = 2, 3, 3
Q_BLK = {"a": 0, "b": 2, "c": 5}
KV_BLK = {"a": 8, "b": 10, "c": 13}
N_BLKS = QKV_WIDTH // LANES

VMEM_LIMIT = 56 * 1024 * 1024


def _cparams(n_axes):
    return pltpu.CompilerParams(
        dimension_semantics=("arbitrary",) * n_axes, vmem_limit_bytes=VMEM_LIMIT)


def _resident(block_shape, index_map):
    return pl.BlockSpec(block_shape, index_map, pipeline_mode=pl.Buffered(1))


def _rms(x, gain):
    ms = jnp.mean(x * x, axis=-1, keepdims=True)
    return x * lax.rsqrt(ms + RMS_EPS) * gain


def _rotate(y, lane, half):
    up = pltpu.roll(y, LANES - half, 1)
    dn = pltpu.roll(y, half, 1)
    return jnp.where((lane % (2 * half)) < half, up, dn)


def _qkv_kernel(x_ref, nrm_ref, w_ref, gain_ref, rope_ref, ones_ref, out_ref, h_ref, *, chunk):
    tm = x_ref.shape[0]
    h_ref[...] = _rms(x_ref[...], nrm_ref[...]).astype(BF16)
    lane = lax.broadcasted_iota(jnp.int32, (tm, LANES), 1)
    is_k = lane < HEAD_DIM
    for c in range(QKV_WIDTH // chunk):
        wide = jnp.dot(h_ref[...], w_ref[:, c * chunk:(c + 1) * chunk], preferred_element_type=F32)
        for hb in range(chunk // LANES):
            if hb % 2 == 0:
                p = wide[:, hb * LANES:hb * LANES + PAIR]
                ss = jnp.dot((p * p).astype(BF16), ones_ref[...], preferred_element_type=F32)
                inv = lax.rsqrt(ss * (1.0 / HEAD_DIM) + RMS_EPS)
            blk = c * (chunk // LANES) + hb
            cols = slice((hb % 2) * LANES, (hb % 2 + 1) * LANES)
            is_q = blk < KV_BLK["a"]
            scale = inv[:, cols] if is_q else jnp.where(is_k, inv[:, cols], 1.0)
            y = p[:, cols] * scale * gain_ref[:, blk * LANES:(blk + 1) * LANES]
            if blk < Q_BLK["b"]:
                tab, half = 0, HEAD_DIM // 4
            elif blk < Q_BLK["c"]:
                tab, half = None, None
            elif is_q:
                tab, half = 4, HEAD_DIM // 2
            elif blk < KV_BLK["b"]:
                tab, half = 2, HEAD_DIM // 4
            elif blk < KV_BLK["c"]:
                tab, half = None, None
            else:
                tab, half = 6, HEAD_DIM // 2
            if tab is not None:
                y = y * rope_ref[tab] + _rotate(y, lane, half) * rope_ref[tab + 1]
            elif is_q:
                y = y * SCALE
            out_ref[:, blk * LANES:(blk + 1) * LANES] = y.astype(BF16)


def _qkv_proj(x, layer, nrm, w_in, gains, rope, ones, tm):
    b, t, d = x.shape
    return pl.pallas_call(
        functools.partial(_qkv_kernel, chunk=4 * LANES),
        grid=(t // tm, b),
        in_specs=[
            pl.BlockSpec((None, tm, d), lambda i, bb: (bb, i, 0)),
            _resident((None, 1, d), lambda i, bb: (layer, 0, 0)),
            _resident((None, d, QKV_WIDTH), lambda i, bb: (layer, 0, 0)),
            _resident((None, 1, QKV_WIDTH), lambda i, bb: (layer, 0, 0)),
            pl.BlockSpec((8, tm, LANES), lambda i, bb: (0, i, 0)),
            _resident((PAIR, PAIR), lambda i, bb: (0, 0)),
        ],
        out_specs=pl.BlockSpec((None, tm, QKV_WIDTH), lambda i, bb: (bb, i, 0)),
        out_shape=jax.ShapeDtypeStruct((b, t, QKV_WIDTH), BF16),
        scratch_shapes=[pltpu.VMEM((tm, d), BF16)],
        compiler_params=_cparams(2),
        name="qkv_proj",
    )(x, nrm, w_in, gains, rope, ones)


def _stack_heads(q):
    q32 = q.astype(F32)
    lane = lax.broadcasted_iota(jnp.int32, q32.shape, 1)
    lo = lane < HEAD_DIM
    q0 = jnp.where(lo, q32, 0.0)
    q1 = jnp.where(lo, pltpu.roll(q32, HEAD_DIM, 1), 0.0)
    return jnp.concatenate([q0, q1], axis=0).astype(BF16)


def _scores(a, b):
    return lax.dot_general(a, b, (((1,), (1,)), ((), ())), preferred_element_type=F32)


def _ones_for_k(kv32):
    lane = lax.broadcasted_iota(jnp.int32, kv32.shape, 1)
    return jnp.where(lane < HEAD_DIM, 1.0, kv32).astype(BF16)


def _split_pv(o2):
    n = o2.shape[0] // 2
    lo = lax.broadcasted_iota(jnp.int32, (n, LANES), 1) < HEAD_DIM
    pv = jnp.where(lo, pltpu.roll(o2[:n], HEAD_DIM, 1), o2[n:])
    den = jnp.where(lo, o2[:n], pltpu.roll(o2[n:], HEAD_DIM, 1))
    return pv, den


def _attn_a_kernel(q_ref, kv_ref, o_ref, kv1_ref, s_ref, *, tk, sub):
    t = kv_ref.shape[0]
    n_sub = q_ref.shape[0] // sub
    w = 2 * sub

    @pl.when(pl.program_id(2) == 0)
    def _():
        kv1_ref[...] = _ones_for_k(kv_ref[...].astype(F32))

    tops = []
    for h in range(n_sub):
        q2 = _stack_heads(q_ref[h * sub:(h + 1) * sub, :])
        run_max = jnp.full((w, LANES), -jnp.inf, F32)
        for j in range(t // tk):
            sc = _scores(q2, kv_ref[j * tk:(j + 1) * tk, :])
            s_ref[h, :, j * tk:(j + 1) * tk] = sc
            for c in range(tk // LANES):
                run_max = jnp.maximum(run_max, sc[:, c * LANES:(c + 1) * LANES])
        tops.append(jnp.max(run_max, axis=-1, keepdims=True))

    for h in range(n_sub):
        acc = jnp.zeros((w, LANES), F32)
        for j in range(t // tk):
            p = jnp.exp2(s_ref[h, :, j * tk:(j + 1) * tk] - tops[h]).astype(BF16)
            acc = acc + jnp.dot(p, kv1_ref[j * tk:(j + 1) * tk, :], preferred_element_type=F32)
        pv, den = _split_pv(acc)
        o_ref[h * sub:(h + 1) * sub, :] = pv / den


def _attn_a(qkv, tq, tk, sub):
    b, t, _ = qkv.shape
    return pl.pallas_call(
        functools.partial(_attn_a_kernel, tk=tk, sub=sub),
        grid=(b, A_GROUPS, t // tq),
        in_specs=[
            pl.BlockSpec((None, tq, LANES), lambda bb, g, i: (bb, i, Q_BLK["a"] + g)),
            pl.BlockSpec((None, t, LANES), lambda bb, g, i: (bb, 0, KV_BLK["a"] + g)),
        ],
        out_specs=pl.BlockSpec((None, tq, LANES), lambda bb, g, i: (bb, i, g)),
        out_shape=jax.ShapeDtypeStruct((b, t, A_GROUPS * LANES), F32),
        scratch_shapes=[pltpu.VMEM((t, LANES), BF16), pltpu.VMEM((tq // sub, 2 * sub, t), F32)],
        compiler_params=_cparams(3),
        name="attn_global",
    )(qkv, qkv)


def _attn_b_kernel(q_ref, kv_ref, bias_ref, o_ref, kv1_ref, *, unroll):
    rows = q_ref.shape[0] // GRID_W
    n_keys = NA_ROWS * GRID_W
    kv1_ref[...] = _ones_for_k(kv_ref[...].astype(F32))

    def body(it, carry):
        offs, scores = [], []
        for u in range(unroll):
            r = it * unroll + u
            r0 = jnp.clip(r - NA_ROWS // 2, 0, rows - NA_ROWS)
            qoff = pl.multiple_of(r * GRID_W, GRID_W)
            koff = pl.multiple_of(r0 * GRID_W, GRID_W)
            q2 = _stack_heads(q_ref[pl.ds(qoff, GRID_W), :])
            scores.append(_scores(q2, kv_ref[pl.ds(koff, n_keys), :])
                          + bias_ref[r0 - r + (NA_ROWS - 1)])
            offs.append((qoff, koff))
        probs = [jnp.exp2(sc - jnp.max(sc, axis=-1, keepdims=True)).astype(BF16) for sc in scores]
        outs = [jnp.dot(p, kv1_ref[pl.ds(koff, n_keys), :], preferred_element_type=F32)
                for p, (_, koff) in zip(probs, offs)]
        for o2, (qoff, _) in zip(outs, offs):
            pv, den = _split_pv(o2)
            o_ref[pl.ds(qoff, GRID_W), :] = pv / den
        return carry

    lax.fori_loop(0, rows // unroll, body, 0)


def _attn_b(qkv, bias, layer, unroll):
    b, t, _ = qkv.shape
    return pl.pallas_call(
        functools.partial(_attn_b_kernel, unroll=unroll),
        grid=(B_GROUPS, b),
        in_specs=[
            pl.BlockSpec((None, t, LANES), lambda g, bb: (bb, 0, Q_BLK["b"] + g)),
            pl.BlockSpec((None, t, LANES), lambda g, bb: (bb, 0, KV_BLK["b"] + g)),
            pl.BlockSpec((None, None, NA_ROWS, 2 * GRID_W, NA_ROWS * GRID_W),
                         lambda g, bb: (layer, g, 0, 0, 0)),
        ],
        out_specs=pl.BlockSpec((None, t, LANES), lambda g, bb: (bb, 0, g)),
        out_shape=jax.ShapeDtypeStruct((b, t, B_GROUPS * LANES), F32),
        scratch_shapes=[pltpu.VMEM((t, LANES), BF16)],
        compiler_params=_cparams(2),
        name="attn_neighbourhood",
    )(qkv, qkv, bias)


def _natural_residue(cls, level):
    res = 0
    for d in range(level):
        res = res + ((cls // (STEP ** (level - 1 - d))) % STEP) * (STEP ** d)
    return res


def _attn_c_kernel(q_ref, kv_ref, bias_ref, o_ref, qf_ref, kvf_ref, qg_ref, kvg_ref, qs_ref,
                   kvs_ref, kv1_ref, ma_ref, wa_ref, na_ref, mb_ref, wb_ref, nb_ref, *, tq,
                   unroll):
    t = q_ref.shape[0]
    lo = lax.broadcasted_iota(jnp.int32, (tq, LANES), 1) < HEAD_DIM
    qf_ref[...] = q_ref[...].astype(F32)
    kvf_ref[...] = kv_ref[...].astype(F32)
    f32_src = (qf_ref, kvf_ref)
    states = ((ma_ref, wa_ref, na_ref), (mb_ref, wb_ref, nb_ref))

    for level, dil in enumerate(DILATIONS):
        length = t // dil
        n_tiles = length // tq
        n_keys = min(length, tq + 2 * HALF_WINDOW)
        first, last = level == 0, level == len(DILATIONS) - 1
        prev_len = length * STEP
        st_in, st_out = states[(level + 1) % 2], states[level % 2]

        if first:
            q_src, kv_src = q_ref, kv_ref
            kv1_ref[...] = _ones_for_k(kvf_ref[...])
        else:
            q_src, kv_src = qs_ref, kvs_ref
            f32_dst = None if last else (qg_ref, kvg_ref)

            def gather_class(c, carry, length=length, prev_len=prev_len, f32_src=f32_src,
                             f32_dst=f32_dst):
                parent = c // STEP
                src = pl.ds(parent * prev_len + (c - parent * STEP), length, stride=STEP)
                rows = pl.ds(pl.multiple_of(c * length, LANES), length)
                qd = f32_src[0][src, :]
                kvd = f32_src[1][src, :]
                qs_ref[rows, :] = qd.astype(BF16)
                kvs_ref[rows, :] = kvd.astype(BF16)
                kv1_ref[rows, :] = _ones_for_k(kvd)
                if f32_dst is not None:
                    f32_dst[0][rows, :] = qd
                    f32_dst[1][rows, :] = kvd
                return carry

            lax.fori_loop(0, dil, gather_class, 0)
            f32_src = f32_dst

        def body(it, carry, level=level, dil=dil, length=length, n_tiles=n_tiles, n_keys=n_keys,
                 first=first, last=last, prev_len=prev_len, q_src=q_src, kv_src=kv_src,
                 st_in=st_in, st_out=st_out):
            where, scores = [], []
            for u in range(unroll):
                g = it * unroll + u
                c = g // n_tiles
                i = g - c * n_tiles
                a0 = i * tq
                start = jnp.clip(a0 - HALF_WINDOW, 0, length - n_keys)
                kind = jnp.where(i == 0, 0, jnp.where(i == n_tiles - 1, 2, 1))
                own = pl.ds(pl.multiple_of(g * tq, tq), tq)
                krow = pl.multiple_of(c * length + start, HALF_WINDOW)
                q2 = _stack_heads(q_src[own, :])
                scores.append(_scores(q2, kv_src[pl.ds(krow, n_keys), :])
                              + bias_ref[kind, :, :n_keys])
                parent = c // STEP
                prev = pl.ds(parent * prev_len + STEP * a0 + (c - parent * STEP), tq, stride=STEP)
                out = pl.ds(dil * a0 + _natural_residue(c, level), tq, stride=dil)
                where.append((krow, own, prev, out))
            tops = [jnp.max(sc, axis=-1, keepdims=True) for sc in scores]
            probs = [jnp.exp2(sc - m).astype(BF16) for sc, m in zip(scores, tops)]
            outs = [jnp.dot(p, kv1_ref[pl.ds(krow, n_keys), :], preferred_element_type=F32)
                    for p, (krow, _, _, _) in zip(probs, where)]
            for o2, m, (_, own, prev, out) in zip(outs, tops, where):
                pv, den = _split_pv(o2)
                top = jnp.where(lo, jnp.broadcast_to(m[:tq], (tq, LANES)),
                                jnp.broadcast_to(m[tq:], (tq, LANES)))
                if not first:
                    m_old = st_in[0][prev, :]
                    m_new = jnp.maximum(m_old, top)
                    e_old = jnp.exp2(m_old - m_new)
                    e_new = jnp.exp2(top - m_new)
                    pv = e_old * st_in[2][prev, :] + e_new * pv
                    den = e_old * st_in[1][prev, :] + e_new * den
                    top = m_new
                if last:
                    o_ref[out, :] = pv / den
                else:
                    st_out[0][own, :] = top
                    st_out[1][own, :] = den
                    st_out[2][own, :] = pv
            return carry

        lax.fori_loop(0, (t // tq) // unroll, body, 0)


def _band_bias(tq):
    row = jnp.arange(2 * tq)[:, None] % tq
    col = jnp.arange(tq + 2 * HALF_WINDOW)[None, :]
    delta = jnp.arange(3)[:, None, None] * HALF_WINDOW
    return jnp.where(jnp.abs(delta + row - col) <= HALF_WINDOW, 0.0, NEG_INF).astype(F32)


def _attn_c(qkv, band, tq, unroll):
    b, t, _ = qkv.shape
    f32 = pltpu.VMEM((t, LANES), F32)
    b16 = pltpu.VMEM((t, LANES), BF16)
    return pl.pallas_call(
        functools.partial(_attn_c_kernel, tq=tq, unroll=unroll),
        grid=(C_GROUPS, b),
        in_specs=[
            pl.BlockSpec((None, t, LANES), lambda g, bb: (bb, 0, Q_BLK["c"] + g)),
            pl.BlockSpec((None, t, LANES), lambda g, bb: (bb, 0, KV_BLK["c"] + g)),
            _resident(band.shape, lambda g, bb: (0, 0, 0)),
        ],
        out_specs=pl.BlockSpec((None, t, LANES), lambda g, bb: (bb, 0, g)),
        out_shape=jax.ShapeDtypeStruct((b, t, C_GROUPS * LANES), F32),
        scratch_shapes=[f32] * 4 + [b16] * 3 + [f32] * 6,
        compiler_params=_cparams(2),
        name="attn_dilated",
    )(qkv, qkv, band)


def _tail_kernel(x_ref, oa_ref, ob_ref, oc_ref, og_ref, wo_ref, nf_ref, wgu_ref, wd_ref,
                 out_ref, y_ref, act_ref, *, fchunk):
    a_w = A_GROUPS * LANES
    b_w = B_GROUPS * LANES
    y_ref[:, :a_w] = _rms(oa_ref[...], og_ref[:, :a_w]).astype(BF16)
    y_ref[:, a_w:a_w + b_w] = _rms(ob_ref[...], og_ref[:, a_w:a_w + b_w]).astype(BF16)
    y_ref[:, a_w + b_w:] = _rms(oc_ref[...], og_ref[:, a_w + b_w:]).astype(BF16)

    x1 = x_ref[...] + jnp.dot(y_ref[...], wo_ref[...], preferred_element_type=F32)
    h = _rms(x1, nf_ref[...]).astype(BF16)
    for c in range(FFN_DIM // fchunk):
        gate = jnp.dot(h, wgu_ref[:, c * fchunk:(c + 1) * fchunk], preferred_element_type=F32)
        up = jnp.dot(h, wgu_ref[:, FFN_DIM + c * fchunk:FFN_DIM + (c + 1) * fchunk],
                     preferred_element_type=F32)
        act_ref[:, c * fchunk:(c + 1) * fchunk] = (jax.nn.silu(gate) * up).astype(BF16)
    out_ref[...] = x1 + jnp.dot(act_ref[...], wd_ref[...], preferred_element_type=F32)


def _tail(x, oa, ob, oc, layer, out_gain, w_out, norm_ffn, w_gu, w_down, tm, fchunk):
    b, t, d = x.shape
    n = b * t
    flat = lambda a: a.reshape(n, a.shape[-1])
    row = lambda w: pl.BlockSpec((tm, w), lambda i: (i, 0))
    out = pl.pallas_call(
        functools.partial(_tail_kernel, fchunk=fchunk),
        grid=(n // tm,),
        in_specs=[
            row(d), row(A_GROUPS * LANES), row(B_GROUPS * LANES), row(C_GROUPS * LANES),
            _resident((None, 1, d), lambda i: (layer, 0, 0)),
            _resident((None, d, d), lambda i: (layer, 0, 0)),
            _resident((None, 1, d), lambda i: (layer, 0, 0)),
            _resident((None, d, 2 * FFN_DIM), lambda i: (layer, 0, 0)),
            _resident((None, FFN_DIM, d), lambda i: (layer, 0, 0)),
        ],
        out_specs=row(d),
        out_shape=jax.ShapeDtypeStruct((n, d), F32),
        scratch_shapes=[pltpu.VMEM((tm, d), BF16), pltpu.VMEM((tm, FFN_DIM), BF16)],
        compiler_params=_cparams(1),
        name="out_proj_ffn",
    )(flat(x), flat(oa), flat(ob), flat(oc),
      out_gain, w_out, norm_ffn, w_gu, w_down)
    return out.reshape(b, t, d)


def _rope_tables(pos, dim):
    inv_freq = ROPE_THETA ** (-jnp.arange(0, dim, 2, dtype=F32) / dim)
    ang = pos[:, None] * inv_freq[None, :]
    ang = jnp.concatenate([ang, ang], axis=-1)
    return jnp.cos(ang), jnp.sin(ang)


def _rope_stack(t):
    pos = jnp.arange(t, dtype=jnp.int32)
    half = HEAD_DIM // 2
    cos_r, sin_r = _rope_tables((pos // GRID_W).astype(F32), half)
    cos_c, sin_c = _rope_tables((pos % GRID_W).astype(F32), half)
    cos_1, sin_1 = _rope_tables(pos.astype(F32), HEAD_DIM)
    j = jnp.arange(HEAD_DIM)
    cos_a = jnp.concatenate([cos_r, cos_c], axis=-1)
    sin_a = jnp.concatenate([sin_r, sin_c], axis=-1) * jnp.where(j % half < half // 2, -1.0, 1.0)
    sin_1 = sin_1 * jnp.where(j < half, -1.0, 1.0)
    one, zero = jnp.ones_like(cos_a), jnp.zeros_like(cos_a)
    cat = lambda u, v: jnp.concatenate([u, v], axis=-1)
    return jnp.stack([
        cat(cos_a, cos_a) * SCALE, cat(sin_a, sin_a) * SCALE, cat(cos_a, one), cat(sin_a, zero),
        cat(cos_1, cos_1) * SCALE, cat(sin_1, sin_1) * SCALE, cat(cos_1, one), cat(sin_1, zero),
    ]).astype(F32)


def _neighbourhood_bias(rpb):
    depth = rpb.shape[0]
    qc = jnp.arange(GRID_W)[:, None]
    kc = jnp.arange(GRID_W)[None, :]
    c0 = jnp.clip(qc - NA_COLS // 2, 0, GRID_W - NA_COLS)
    valid = (kc >= c0) & (kc < c0 + NA_COLS)
    dc = jnp.clip(kc - qc + NA_COLS - 1, 0, 2 * NA_COLS - 2)
    dr = jnp.arange(NA_ROWS)[:, None] + jnp.arange(NA_ROWS)[None, :]
    tab = (rpb * LOG2E)[:, :, dr][:, :, :, :, dc]
    tab = jnp.where(valid[None, None, None, None], tab, NEG_INF)
    tab = jnp.transpose(tab, (0, 1, 2, 4, 3, 5))
    tab = tab.reshape(depth, B_GROUPS, 2, NA_ROWS, GRID_W, NA_ROWS * GRID_W)
    tab = jnp.transpose(tab, (0, 1, 3, 2, 4, 5))
    return tab.reshape(depth, B_GROUPS, NA_ROWS, 2 * GRID_W, NA_ROWS * GRID_W).astype(F32)


def _prepare(norm_mix, w_in, q_gain, k_gain, rpb, out_gain, w_out, norm_ffn, w_gate_up, w_down):
    depth = w_in.shape[0]
    q_w = N_HEADS * HEAD_DIM
    kv_w = KV_HEADS * HEAD_DIM
    wq = w_in[..., :q_w]
    wk = w_in[..., q_w:q_w + kv_w].reshape(depth, D_MODEL, KV_HEADS, 1, HEAD_DIM)
    wv = w_in[..., q_w + kv_w:].reshape(depth, D_MODEL, KV_HEADS, 1, HEAD_DIM)
    wkv = jnp.concatenate([wk, wv], axis=3).reshape(depth, D_MODEL, 2 * kv_w)
    w_qkv = jnp.concatenate([wq, wkv], axis=-1).astype(BF16)

    heads_q = (4, 6, 6)
    heads_kv = (2, 3, 3)
    gq = jnp.concatenate([jnp.tile(q_gain[:, m], (1, heads_q[m])) for m in range(3)], axis=-1)
    ones = jnp.ones((depth, HEAD_DIM), F32)
    gkv = jnp.concatenate(
        [jnp.tile(jnp.concatenate([k_gain[:, m], ones], axis=-1), (1, heads_kv[m]))
         for m in range(3)], axis=-1)
    gains = jnp.concatenate([gq, gkv], axis=-1).reshape(depth, 1, QKV_WIDTH).astype(F32)

    seg = jnp.arange(PAIR) // HEAD_DIM
    head_ones = (seg[:, None] == seg[None, :]).astype(BF16)
    return dict(
        norm_mix=norm_mix.reshape(depth, 1, D_MODEL), w_qkv=w_qkv, gains=gains,
        head_ones=head_ones, out_gain=out_gain.reshape(depth, 1, D_MODEL),
        w_out=w_out.astype(BF16), norm_ffn=norm_ffn.reshape(depth, 1, D_MODEL),
        w_gu=w_gate_up.astype(BF16), w_down=w_down.astype(BF16),
        bias=_neighbourhood_bias(rpb), band=_band_bias(TQ_C))


def _trunk(x, prm, *, tm_qkv=512, tq_a=512, tk_a=512, sub_a=128, unroll_b=16, unroll_c=8,
           tm_tail=256, fchunk=256):
    b, t, _ = x.shape
    depth = prm["w_qkv"].shape[0]
    rope = _rope_stack(t)
    bias, band, tq_c = prm["bias"], prm["band"], TQ_C
    for layer in range(depth):
        qkv = _qkv_proj(x, layer, prm["norm_mix"], prm["w_qkv"], prm["gains"], rope,
                        prm["head_ones"], tm_qkv)
        oa = _attn_a(qkv, tq_a, tk_a, sub_a)
        ob = _attn_b(qkv, bias, layer, unroll_b)
        oc = _attn_c(qkv, band, tq_c, unroll_c)
        x = _tail(x, oa, ob, oc, layer, prm["out_gain"], prm["w_out"], prm["norm_ffn"],
                  prm["w_gu"], prm["w_down"], tm_tail, fchunk)
    return x


def kernel(x_prompt, x_sample, norm_mix, w_in, q_gain, k_gain, rpb, out_gain, w_out, norm_ffn,
           w_gate_up, w_down):
    prm = _prepare(norm_mix, w_in, q_gain, k_gain, rpb, out_gain, w_out, norm_ffn, w_gate_up,
                   w_down)
    return (_trunk(x_prompt, prm), _trunk(x_sample, prm))
```

```python
import functools

import jax
import jax.numpy as jnp
from jax import lax
from jax.experimental import pallas as pl
from jax.experimental.pallas import tpu as pltpu

F32 = jnp.float32
BF16 = jnp.bfloat16

D_MODEL = 1024
HEAD_DIM = 64
N_HEADS = 16
KV_HEADS = 8
QKV_WIDTH = 2048
FFN_DIM = 2816
GRID_W = 64
NA_ROWS = 8
NA_COLS = 16
STEP = 4
DILATIONS = (1, STEP, STEP * STEP)
HALF_WINDOW = 64
TQ_C = 2 * HALF_WINDOW
ROPE_THETA = 10000.0
RMS_EPS = 1e-6
NEG_INF = -1e30
LOG2E = 1.4426950408889634
SCALE = HEAD_DIM ** -0.5 * LOG2E

LANES = 128
PAIR = 2 * LANES
A_GROUPS, B_GROUPS, C_GROUPS = 2, 3, 3
Q_BLK = {"a": 0, "b": 2, "c": 5}
KV_BLK = {"a": 8, "b": 10, "c": 13}
N_BLKS = QKV_WIDTH // LANES

VMEM_LIMIT = 56 * 1024 * 1024


def _cparams(n_axes):
    return pltpu.CompilerParams(
        dimension_semantics=("arbitrary",) * n_axes, vmem_limit_bytes=VMEM_LIMIT)


def _resident(block_shape, index_map):
    return pl.BlockSpec(block_shape, index_map, pipeline_mode=pl.Buffered(1))


def _rms(x, gain):
    ms = jnp.mean(x * x, axis=-1, keepdims=True)
    return x * lax.rsqrt(ms + RMS_EPS) * gain


def _rotate(y, lane, half):
    up = pltpu.roll(y, LANES - half, 1)
    dn = pltpu.roll(y, half, 1)
    return jnp.where((lane % (2 * half)) < half, up, dn)


def _qkv_kernel(x_ref, nrm_ref, w_ref, gain_ref, rope_ref, ones_ref, out_ref, h_ref, *, chunk):
    tm = x_ref.shape[0]
    h_ref[...] = _rms(x_ref[...], nrm_ref[...]).astype(BF16)
    lane = lax.broadcasted_iota(jnp.int32, (tm, LANES), 1)
    is_k = lane < HEAD_DIM
    for c in range(QKV_WIDTH // chunk):
        wide = jnp.dot(h_ref[...], w_ref[:, c * chunk:(c + 1) * chunk], preferred_element_type=F32)
        for hb in range(chunk // LANES):
            if hb % 2 == 0:
                p = wide[:, hb * LANES:hb * LANES + PAIR]
                ss = jnp.dot((p * p).astype(BF16), ones_ref[...], preferred_element_type=F32)
                inv = lax.rsqrt(ss * (1.0 / HEAD_DIM) + RMS_EPS)
            blk = c * (chunk // LANES) + hb
            cols = slice((hb % 2) * LANES, (hb % 2 + 1) * LANES)
            is_q = blk < KV_BLK["a"]
            scale = inv[:, cols] if is_q else jnp.where(is_k, inv[:, cols], 1.0)
            y = p[:, cols] * scale * gain_ref[:, blk * LANES:(blk + 1) * LANES]
            if blk < Q_BLK["b"]:
                tab, half = 0, HEAD_DIM // 4
            elif blk < Q_BLK["c"]:
                tab, half = None, None
            elif is_q:
                tab, half = 4, HEAD_DIM // 2
            elif blk < KV_BLK["b"]:
                tab, half = 2, HEAD_DIM // 4
            elif blk < KV_BLK["c"]:
                tab, half = None, None
            else:
                tab, half = 6, HEAD_DIM // 2
            if tab is not None:
                y = y * rope_ref[tab] + _rotate(y, lane, half) * rope_ref[tab + 1]
            elif is_q:
                y = y * SCALE
            out_ref[:, blk * LANES:(blk + 1) * LANES] = y.astype(BF16)


def _qkv_proj(x, layer, nrm, w_in, gains, rope, ones, tm):
    b, t, d = x.shape
    return pl.pallas_call(
        functools.partial(_qkv_kernel, chunk=4 * LANES),
        grid=(t // tm, b),
        in_specs=[
            pl.BlockSpec((None, tm, d), lambda i, bb: (bb, i, 0)),
            _resident((None, 1, d), lambda i, bb: (layer, 0, 0)),
            _resident((None, d, QKV_WIDTH), lambda i, bb: (layer, 0, 0)),
            _resident((None, 1, QKV_WIDTH), lambda i, bb: (layer, 0, 0)),
            pl.BlockSpec((8, tm, LANES), lambda i, bb: (0, i, 0)),
            _resident((PAIR, PAIR), lambda i, bb: (0, 0)),
        ],
        out_specs=pl.BlockSpec((None, tm, QKV_WIDTH), lambda i, bb: (bb, i, 0)),
        out_shape=jax.ShapeDtypeStruct((b, t, QKV_WIDTH), BF16),
        scratch_shapes=[pltpu.VMEM((tm, d), BF16)],
        compiler_params=_cparams(2),
        name="qkv_proj",
    )(x, nrm, w_in, gains, rope, ones)


def _stack_heads(q):
    q32 = q.astype(F32)
    lane = lax.broadcasted_iota(jnp.int32, q32.shape, 1)
    lo = lane < HEAD_DIM
    q0 = jnp.where(lo, q32, 0.0)
    q1 = jnp.where(lo, pltpu.roll(q32, HEAD_DIM, 1), 0.0)
    return jnp.concatenate([q0, q1], axis=0).astype(BF16)


def _scores(a, b):
    return lax.dot_general(a, b, (((1,), (1,)), ((), ())), preferred_element_type=F32)


def _ones_for_k(kv32):
    lane = lax.broadcasted_iota(jnp.int32, kv32.shape, 1)
    return jnp.where(lane < HEAD_DIM, 1.0, kv32).astype(BF16)


def _split_pv(o2):
    n = o2.shape[0] // 2
    lo = lax.broadcasted_iota(jnp.int32, (n, LANES), 1) < HEAD_DIM
    pv = jnp.where(lo, pltpu.roll(o2[:n], HEAD_DIM, 1), o2[n:])
    den = jnp.where(lo, o2[:n], pltpu.roll(o2[n:], HEAD_DIM, 1))
    return pv, den


def _attn_a_kernel(q_ref, kv_ref, o_ref, kv1_ref, s_ref, *, tk, sub):
    t = kv_ref.shape[0]
    n_sub = q_ref.shape[0] // sub
    w = 2 * sub

    @pl.when(pl.program_id(2) == 0)
    def _():
        kv1_ref[...] = _ones_for_k(kv_ref[...].astype(F32))

    tops = []
    for h in range(n_sub):
        q2 = _stack_heads(q_ref[h * sub:(h + 1) * sub, :])
        run_max = jnp.full((w, LANES), -jnp.inf, F32)
        for j in range(t // tk):
            sc = _scores(q2, kv_ref[j * tk:(j + 1) * tk, :])
            s_ref[h, :, j * tk:(j + 1) * tk] = sc
            for c in range(tk // LANES):
                run_max = jnp.maximum(run_max, sc[:, c * LANES:(c + 1) * LANES])
        tops.append(jnp.max(run_max, axis=-1, keepdims=True))

    for h in range(n_sub):
        acc = jnp.zeros((w, LANES), F32)
        for j in range(t // tk):
            p = jnp.exp2(s_ref[h, :, j * tk:(j + 1) * tk] - tops[h]).astype(BF16)
            acc = acc + jnp.dot(p, kv1_ref[j * tk:(j + 1) * tk, :], preferred_element_type=F32)
        pv, den = _split_pv(acc)
        o_ref[h * sub:(h + 1) * sub, :] = pv / den


def _attn_a(qkv, tq, tk, sub):
    b, t, _ = qkv.shape
    return pl.pallas_call(
        functools.partial(_attn_a_kernel, tk=tk, sub=sub),
        grid=(b, A_GROUPS, t // tq),
        in_specs=[
            pl.BlockSpec((None, tq, LANES), lambda bb, g, i: (bb, i, Q_BLK["a"] + g)),
            pl.BlockSpec((None, t, LANES), lambda bb, g, i: (bb, 0, KV_BLK["a"] + g)),
        ],
        out_specs=pl.BlockSpec((None, tq, LANES), lambda bb, g, i: (bb, i, g)),
        out_shape=jax.ShapeDtypeStruct((b, t, A_GROUPS * LANES), F32),
        scratch_shapes=[pltpu.VMEM((t, LANES), BF16), pltpu.VMEM((tq // sub, 2 * sub, t), F32)],
        compiler_params=_cparams(3),
        name="attn_global",
    )(qkv, qkv)


def _attn_b_kernel(q_ref, kv_ref, bias_ref, o_ref, kv1_ref, *, unroll):
    rows = q_ref.shape[0] // GRID_W
    n_keys = NA_ROWS * GRID_W
    kv1_ref[...] = _ones_for_k(kv_ref[...].astype(F32))

    def body(it, carry):
        offs, scores = [], []
        for u in range(unroll):
            r = it * unroll + u
            r0 = jnp.clip(r - NA_ROWS // 2, 0, rows - NA_ROWS)
            qoff = pl.multiple_of(r * GRID_W, GRID_W)
            koff = pl.multiple_of(r0 * GRID_W, GRID_W)
            q2 = _stack_heads(q_ref[pl.ds(qoff, GRID_W), :])
            scores.append(_scores(q2, kv_ref[pl.ds(koff, n_keys), :])
                          + bias_ref[r0 - r + (NA_ROWS - 1)])
            offs.append((qoff, koff))
        probs = [jnp.exp2(sc - jnp.max(sc, axis=-1, keepdims=True)).astype(BF16) for sc in scores]
        outs = [jnp.dot(p, kv1_ref[pl.ds(koff, n_keys), :], preferred_element_type=F32)
                for p, (_, koff) in zip(probs, offs)]
        for o2, (qoff, _) in zip(outs, offs):
            pv, den = _split_pv(o2)
            o_ref[pl.ds(qoff, GRID_W), :] = pv / den
        return carry

    lax.fori_loop(0, rows // unroll, body, 0)


def _attn_b(qkv, bias, layer, unroll):
    b, t, _ = qkv.shape
    return pl.pallas_call(
        functools.partial(_attn_b_kernel, unroll=unroll),
        grid=(B_GROUPS, b),
        in_specs=[
            pl.BlockSpec((None, t, LANES), lambda g, bb: (bb, 0, Q_BLK["b"] + g)),
            pl.BlockSpec((None, t, LANES), lambda g, bb: (bb, 0, KV_BLK["b"] + g)),
            pl.BlockSpec((None, None, NA_ROWS, 2 * GRID_W, NA_ROWS * GRID_W),
                         lambda g, bb: (layer, g, 0, 0, 0)),
        ],
        out_specs=pl.BlockSpec((None, t, LANES), lambda g, bb: (bb, 0, g)),
        out_shape=jax.ShapeDtypeStruct((b, t, B_GROUPS * LANES), F32),
        scratch_shapes=[pltpu.VMEM((t, LANES), BF16)],
        compiler_params=_cparams(2),
        name="attn_neighbourhood",
    )(qkv, qkv, bias)


def _natural_residue(cls, level):
    res = 0
    for d in range(level):
        res = res + ((cls // (STEP ** (level - 1 - d))) % STEP) * (STEP ** d)
    return res


def _attn_c_kernel(q_ref, kv_ref, bias_ref, o_ref, qf_ref, kvf_ref, qg_ref, kvg_ref, qs_ref,
                   kvs_ref, kv1_ref, ma_ref, wa_ref, na_ref, mb_ref, wb_ref, nb_ref, *, tq,
                   unroll):
    t = q_ref.shape[0]
    lo = lax.broadcasted_iota(jnp.int32, (tq, LANES), 1) < HEAD_DIM
    qf_ref[...] = q_ref[...].astype(F32)
    kvf_ref[...] = kv_ref[...].astype(F32)
    f32_src = (qf_ref, kvf_ref)
    states = ((ma_ref, wa_ref, na_ref), (mb_ref, wb_ref, nb_ref))

    for level, dil in enumerate(DILATIONS):
        length = t // dil
        n_tiles = length // tq
        n_keys = min(length, tq + 2 * HALF_WINDOW)
        first, last = level == 0, level == len(DILATIONS) - 1
        prev_len = length * STEP
        st_in, st_out = states[(level + 1) % 2], states[level % 2]

        if first:
            q_src, kv_src = q_ref, kv_ref
            kv1_ref[...] = _ones_for_k(kvf_ref[...])
        else:
            q_src, kv_src = qs_ref, kvs_ref
            f32_dst = None if last else (qg_ref, kvg_ref)

            def gather_class(c, carry, length=length, prev_len=prev_len, f32_src=f32_src,
                             f32_dst=f32_dst):
                parent = c // STEP
                src = pl.ds(parent * prev_len + (c - parent * STEP), length, stride=STEP)
                rows = pl.ds(pl.multiple_of(c * length, LANES), length)
                qd = f32_src[0][src, :]
                kvd = f32_src[1][src, :]
                qs_ref[rows, :] = qd.astype(BF16)
                kvs_ref[rows, :] = kvd.astype(BF16)
                kv1_ref[rows, :] = _ones_for_k(kvd)
                if f32_dst is not None:
                    f32_dst[0][rows, :] = qd
                    f32_dst[1][rows, :] = kvd
                return carry

            lax.fori_loop(0, dil, gather_class, 0)
            f32_src = f32_dst

        def body(it, carry, level=level, dil=dil, length=length, n_tiles=n_tiles, n_keys=n_keys,
                 first=first, last=last, prev_len=prev_len, q_src=q_src, kv_src=kv_src,
                 st_in=st_in, st_out=st_out):
            where, scores = [], []
            for u in range(unroll):
                g = it * unroll + u
                c = g // n_tiles
                i = g - c * n_tiles
                a0 = i * tq
                start = jnp.clip(a0 - HALF_WINDOW, 0, length - n_keys)
                kind = jnp.where(i == 0, 0, jnp.where(i == n_tiles - 1, 2, 1))
                own = pl.ds(pl.multiple_of(g * tq, tq), tq)
                krow = pl.multiple_of(c * length + start, HALF_WINDOW)
                q2 = _stack_heads(q_src[own, :])
                scores.append(_scores(q2, kv_src[pl.ds(krow, n_keys), :])
                              + bias_ref[kind, :, :n_keys])
                parent = c // STEP
                prev = pl.ds(parent * prev_len + STEP * a0 + (c - parent * STEP), tq, stride=STEP)
                out = pl.ds(dil * a0 + _natural_residue(c, level), tq, stride=dil)
                where.append((krow, own, prev, out))
            tops = [jnp.max(sc, axis=-1, keepdims=True) for sc in scores]
            probs = [jnp.exp2(sc - m).astype(BF16) for sc, m in zip(scores, tops)]
            outs = [jnp.dot(p, kv1_ref[pl.ds(krow, n_keys), :], preferred_element_type=F32)
                    for p, (krow, _, _, _) in zip(probs, where)]
            for o2, m, (_, own, prev, out) in zip(outs, tops, where):
                pv, den = _split_pv(o2)
                top = jnp.where(lo, jnp.broadcast_to(m[:tq], (tq, LANES)),
                                jnp.broadcast_to(m[tq:], (tq, LANES)))
                if not first:
                    m_old = st_in[0][prev, :]
                    m_new = jnp.maximum(m_old, top)
                    e_old = jnp.exp2(m_old - m_new)
                    e_new = jnp.exp2(top - m_new)
                    pv = e_old * st_in[2][prev, :] + e_new * pv
                    den = e_old * st_in[1][prev, :] + e_new * den
                    top = m_new
                if last:
                    o_ref[out, :] = pv / den
                else:
                    st_out[0][own, :] = top
                    st_out[1][own, :] = den
                    st_out[2][own, :] = pv
            return carry

        lax.fori_loop(0, (t // tq) // unroll, body, 0)


def _band_bias(tq):
    row = jnp.arange(2 * tq)[:, None] % tq
    col = jnp.arange(tq + 2 * HALF_WINDOW)[None, :]
    delta = jnp.arange(3)[:, None, None] * HALF_WINDOW
    return jnp.where(jnp.abs(delta + row - col) <= HALF_WINDOW, 0.0, NEG_INF).astype(F32)


def _attn_c(qkv, band, tq, unroll):
    b, t, _ = qkv.shape
    f32 = pltpu.VMEM((t, LANES), F32)
    b16 = pltpu.VMEM((t, LANES), BF16)
    return pl.pallas_call(
        functools.partial(_attn_c_kernel, tq=tq, unroll=unroll),
        grid=(C_GROUPS, b),
        in_specs=[
            pl.BlockSpec((None, t, LANES), lambda g, bb: (bb, 0, Q_BLK["c"] + g)),
            pl.BlockSpec((None, t, LANES), lambda g, bb: (bb, 0, KV_BLK["c"] + g)),
            _resident(band.shape, lambda g, bb: (0, 0, 0)),
        ],
        out_specs=pl.BlockSpec((None, t, LANES), lambda g, bb: (bb, 0, g)),
        out_shape=jax.ShapeDtypeStruct((b, t, C_GROUPS * LANES), F32),
        scratch_shapes=[f32] * 4 + [b16] * 3 + [f32] * 6,
        compiler_params=_cparams(2),
        name="attn_dilated",
    )(qkv, qkv, band)


def _tail_kernel(x_ref, oa_ref, ob_ref, oc_ref, og_ref, wo_ref, nf_ref, wgu_ref, wd_ref,
                 out_ref, y_ref, act_ref, *, fchunk):
    a_w = A_GROUPS * LANES
    b_w = B_GROUPS * LANES
    y_ref[:, :a_w] = _rms(oa_ref[...], og_ref[:, :a_w]).astype(BF16)
    y_ref[:, a_w:a_w + b_w] = _rms(ob_ref[...], og_ref[:, a_w:a_w + b_w]).astype(BF16)
    y_ref[:, a_w + b_w:] = _rms(oc_ref[...], og_ref[:, a_w + b_w:]).astype(BF16)

    x1 = x_ref[...] + jnp.dot(y_ref[...], wo_ref[...], preferred_element_type=F32)
    h = _rms(x1, nf_ref[...]).astype(BF16)
    for c in range(FFN_DIM // fchunk):
        gate = jnp.dot(h, wgu_ref[:, c * fchunk:(c + 1) * fchunk], preferred_element_type=F32)
        up = jnp.dot(h, wgu_ref[:, FFN_DIM + c * fchunk:FFN_DIM + (c + 1) * fchunk],
                     preferred_element_type=F32)
        act_ref[:, c * fchunk:(c + 1) * fchunk] = (jax.nn.silu(gate) * up).astype(BF16)
    out_ref[...] = x1 + jnp.dot(act_ref[...], wd_ref[...], preferred_element_type=F32)


def _tail(x, oa, ob, oc, layer, out_gain, w_out, norm_ffn, w_gu, w_down, tm, fchunk):
    b, t, d = x.shape
    n = b * t
    flat = lambda a: a.reshape(n, a.shape[-1])
    row = lambda w: pl.BlockSpec((tm, w), lambda i: (i, 0))
    out = pl.pallas_call(
        functools.partial(_tail_kernel, fchunk=fchunk),
        grid=(n // tm,),
        in_specs=[
            row(d), row(A_GROUPS * LANES), row(B_GROUPS * LANES), row(C_GROUPS * LANES),
            _resident((None, 1, d), lambda i: (layer, 0, 0)),
            _resident((None, d, d), lambda i: (layer, 0, 0)),
            _resident((None, 1, d), lambda i: (layer, 0, 0)),
            _resident((None, d, 2 * FFN_DIM), lambda i: (layer, 0, 0)),
            _resident((None, FFN_DIM, d), lambda i: (layer, 0, 0)),
        ],
        out_specs=row(d),
        out_shape=jax.ShapeDtypeStruct((n, d), F32),
        scratch_shapes=[pltpu.VMEM((tm, d), BF16), pltpu.VMEM((tm, FFN_DIM), BF16)],
        compiler_params=_cparams(1),
        name="out_proj_ffn",
    )(flat(x), flat(oa), flat(ob), flat(oc),
      out_gain, w_out, norm_ffn, w_gu, w_down)
    return out.reshape(b, t, d)


def _rope_tables(pos, dim):
    inv_freq = ROPE_THETA ** (-jnp.arange(0, dim, 2, dtype=F32) / dim)
    ang = pos[:, None] * inv_freq[None, :]
    ang = jnp.concatenate([ang, ang], axis=-1)
    return jnp.cos(ang), jnp.sin(ang)


def _rope_stack(t):
    pos = jnp.arange(t, dtype=jnp.int32)
    half = HEAD_DIM // 2
    cos_r, sin_r = _rope_tables((pos // GRID_W).astype(F32), half)
    cos_c, sin_c = _rope_tables((pos % GRID_W).astype(F32), half)
    cos_1, sin_1 = _rope_tables(pos.astype(F32), HEAD_DIM)
    j = jnp.arange(HEAD_DIM)
    cos_a = jnp.concatenate([cos_r, cos_c], axis=-1)
    sin_a = jnp.concatenate([sin_r, sin_c], axis=-1) * jnp.where(j % half < half // 2, -1.0, 1.0)
    sin_1 = sin_1 * jnp.where(j < half, -1.0, 1.0)
    one, zero = jnp.ones_like(cos_a), jnp.zeros_like(cos_a)
    cat = lambda u, v: jnp.concatenate([u, v], axis=-1)
    return jnp.stack([
        cat(cos_a, cos_a) * SCALE, cat(sin_a, sin_a) * SCALE, cat(cos_a, one), cat(sin_a, zero),
        cat(cos_1, cos_1) * SCALE, cat(sin_1, sin_1) * SCALE, cat(cos_1, one), cat(sin_1, zero),
    ]).astype(F32)


def _neighbourhood_bias(rpb):
    depth = rpb.shape[0]
    qc = jnp.arange(GRID_W)[:, None]
    kc = jnp.arange(GRID_W)[None, :]
    c0 = jnp.clip(qc - NA_COLS // 2, 0, GRID_W - NA_COLS)
    valid = (kc >= c0) & (kc < c0 + NA_COLS)
    dc = jnp.clip(kc - qc + NA_COLS - 1, 0, 2 * NA_COLS - 2)
    band = jnp.where(valid, (rpb * LOG2E)[..., dc], NEG_INF)
    rows = jnp.stack([band[:, :, i:i + NA_ROWS] for i in range(NA_ROWS)], axis=4)
    rows = rows.reshape(depth, B_GROUPS, 2, NA_ROWS, GRID_W, NA_ROWS * GRID_W)
    rows = jnp.swapaxes(rows, 2, 3)
    return rows.reshape(depth, B_GROUPS, NA_ROWS, 2 * GRID_W, NA_ROWS * GRID_W).astype(F32)


def _prepare(norm_mix, w_in, q_gain, k_gain, rpb, out_gain, w_out, norm_ffn, w_gate_up, w_down):
    depth = w_in.shape[0]
    q_w = N_HEADS * HEAD_DIM
    kv_w = KV_HEADS * HEAD_DIM
    wq = w_in[..., :q_w]
    wk = w_in[..., q_w:q_w + kv_w].reshape(depth, D_MODEL, KV_HEADS, 1, HEAD_DIM)
    wv = w_in[..., q_w + kv_w:].reshape(depth, D_MODEL, KV_HEADS, 1, HEAD_DIM)
    wkv = jnp.concatenate([wk, wv], axis=3).reshape(depth, D_MODEL, 2 * kv_w)
    w_qkv = jnp.concatenate([wq, wkv], axis=-1).astype(BF16)

    heads_q = (4, 6, 6)
    heads_kv = (2, 3, 3)
    gq = jnp.concatenate([jnp.tile(q_gain[:, m], (1, heads_q[m])) for m in range(3)], axis=-1)
    ones = jnp.ones((depth, HEAD_DIM), F32)
    gkv = jnp.concatenate(
        [jnp.tile(jnp.concatenate([k_gain[:, m], ones], axis=-1), (1, heads_kv[m]))
         for m in range(3)], axis=-1)
    gains = jnp.concatenate([gq, gkv], axis=-1).reshape(depth, 1, QKV_WIDTH).astype(F32)

    seg = jnp.arange(PAIR) // HEAD_DIM
    head_ones = (seg[:, None] == seg[None, :]).astype(BF16)
    return dict(
        norm_mix=norm_mix.reshape(depth, 1, D_MODEL), w_qkv=w_qkv, gains=gains,
        head_ones=head_ones, out_gain=out_gain.reshape(depth, 1, D_MODEL),
        w_out=w_out.astype(BF16), norm_ffn=norm_ffn.reshape(depth, 1, D_MODEL),
        w_gu=w_gate_up.astype(BF16), w_down=w_down.astype(BF16),
        bias=_neighbourhood_bias(rpb), band=_band_bias(TQ_C))


def _trunk(x, prm, *, tm_qkv=1024, tq_a=512, tk_a=512, sub_a=128, unroll_b=32, unroll_c=16,
           tm_tail=512, fchunk=256):
    b, t, _ = x.shape
    depth = prm["w_qkv"].shape[0]
    rope = _rope_stack(t)
    bias, band, tq_c = prm["bias"], prm["band"], TQ_C
    for layer in range(depth):
        qkv = _qkv_proj(x, layer, prm["norm_mix"], prm["w_qkv"], prm["gains"], rope,
                        prm["head_ones"], tm_qkv)
        oa = _attn_a(qkv, tq_a, tk_a, sub_a)
        ob = _attn_b(qkv, bias, layer, unroll_b)
        oc = _attn_c(qkv, band, tq_c, unroll_c)
        x = _tail(x, oa, ob, oc, layer, prm["out_gain"], prm["w_out"], prm["norm_ffn"],
                  prm["w_gu"], prm["w_down"], tm_tail, fchunk)
    return x


def kernel(x_prompt, x_sample, norm_mix, w_in, q_gain, k_gain, rpb, out_gain, w_out, norm_ffn,
           w_gate_up, w_down):
    prm = _prepare(norm_mix, w_in, q_gain, k_gain, rpb, out_gain, w_out, norm_ffn, w_gate_up,
                   w_down)
    return (_trunk(x_prompt, prm), _trunk(x_sample, prm))
```

```python
import functools

import jax
import jax.numpy as jnp
from jax import lax
from jax.experimental import pallas as pl
from jax.experimental.pallas import tpu as pltpu

F32 = jnp.float32
BF16 = jnp.bfloat16

D_MODEL = 1024
HEAD_DIM = 64
N_HEADS = 16
KV_HEADS = 8
QKV_WIDTH = 2048
FFN_DIM = 2816
GRID_W = 64
NA_ROWS = 8
NA_COLS = 16
STEP = 4
DILATIONS = (1, STEP, STEP * STEP)
HALF_WINDOW = 64
TQ_C = 2 * HALF_WINDOW
ROPE_THETA = 10000.0
RMS_EPS = 1e-6
NEG_INF = -1e30
LOG2E = 1.4426950408889634
SCALE = HEAD_DIM ** -0.5 * LOG2E

LANES = 128
PAIR = 2 * LANES
A_GROUPS, B_GROUPS, C_GROUPS = 2, 3, 3
Q_BLK = {"a": 0, "b": 2, "c": 5}
KV_BLK = {"a": 8, "b": 10, "c": 13}
N_BLKS = QKV_WIDTH // LANES

VMEM_LIMIT = 56 * 1024 * 1024


def _cparams(n_axes):
    return pltpu.CompilerParams(
        dimension_semantics=("arbitrary",) * n_axes, vmem_limit_bytes=VMEM_LIMIT)


def _resident(block_shape, index_map):
    return pl.BlockSpec(block_shape, index_map, pipeline_mode=pl.Buffered(1))


def _rms(x, gain):
    ms = jnp.mean(x * x, axis=-1, keepdims=True)
    return x * lax.rsqrt(ms + RMS_EPS) * gain


def _rotate(y, lane, half):
    up = pltpu.roll(y, LANES - half, 1)
    dn = pltpu.roll(y, half, 1)
    return jnp.where((lane % (2 * half)) < half, up, dn)


def _qkv_kernel(x_ref, nrm_ref, w_ref, gain_ref, rope_ref, ones_ref, out_ref, h_ref, *, chunk):
    tm = x_ref.shape[0]
    h_ref[...] = _rms(x_ref[...], nrm_ref[...]).astype(BF16)
    lane = lax.broadcasted_iota(jnp.int32, (tm, LANES), 1)
    is_k = lane < HEAD_DIM
    for c in range(QKV_WIDTH // chunk):
        wide = jnp.dot(h_ref[...], w_ref[:, c * chunk:(c + 1) * chunk], preferred_element_type=F32)
        for hb in range(chunk // LANES):
            if hb % 2 == 0:
                p = wide[:, hb * LANES:hb * LANES + PAIR]
                ss = jnp.dot((p * p).astype(BF16), ones_ref[...], preferred_element_type=F32)
                inv = lax.rsqrt(ss * (1.0 / HEAD_DIM) + RMS_EPS)
            blk = c * (chunk // LANES) + hb
            cols = slice((hb % 2) * LANES, (hb % 2 + 1) * LANES)
            is_q = blk < KV_BLK["a"]
            scale = inv[:, cols] if is_q else jnp.where(is_k, inv[:, cols], 1.0)
            y = p[:, cols] * scale * gain_ref[:, blk * LANES:(blk + 1) * LANES]
            if blk < Q_BLK["b"]:
                tab, half = 0, HEAD_DIM // 4
            elif blk < Q_BLK["c"]:
                tab, half = None, None
            elif is_q:
                tab, half = 4, HEAD_DIM // 2
            elif blk < KV_BLK["b"]:
                tab, half = 2, HEAD_DIM // 4
            elif blk < KV_BLK["c"]:
                tab, half = None, None
            else:
                tab, half = 6, HEAD_DIM // 2
            if tab is not None:
                y = y * rope_ref[tab] + _rotate(y, lane, half) * rope_ref[tab + 1]
            elif is_q:
                y = y * SCALE
            out_ref[:, blk * LANES:(blk + 1) * LANES] = y.astype(BF16)


def _qkv_proj(x, layer, nrm, w_in, gains, rope, ones, tm):
    b, t, d = x.shape
    return pl.pallas_call(
        functools.partial(_qkv_kernel, chunk=4 * LANES),
        grid=(t // tm, b),
        in_specs=[
            pl.BlockSpec((None, tm, d), lambda i, bb: (bb, i, 0)),
            _resident((None, 1, d), lambda i, bb: (layer, 0, 0)),
            _resident((None, d, QKV_WIDTH), lambda i, bb: (layer, 0, 0)),
            _resident((None, 1, QKV_WIDTH), lambda i, bb: (layer, 0, 0)),
            pl.BlockSpec((8, tm, LANES), lambda i, bb: (0, i, 0)),
            _resident((PAIR, PAIR), lambda i, bb: (0, 0)),
        ],
        out_specs=pl.BlockSpec((None, tm, QKV_WIDTH), lambda i, bb: (bb, i, 0)),
        out_shape=jax.ShapeDtypeStruct((b, t, QKV_WIDTH), BF16),
        scratch_shapes=[pltpu.VMEM((tm, d), BF16)],
        compiler_params=_cparams(2),
        name="qkv_proj",
    )(x, nrm, w_in, gains, rope, ones)


def _stack_heads(q):
    q32 = q.astype(F32)
    lane = lax.broadcasted_iota(jnp.int32, q32.shape, 1)
    lo = lane < HEAD_DIM
    q0 = jnp.where(lo, q32, 0.0)
    q1 = jnp.where(lo, pltpu.roll(q32, HEAD_DIM, 1), 0.0)
    return jnp.concatenate([q0, q1], axis=0).astype(BF16)


def _scores(a, b):
    return lax.dot_general(a, b, (((1,), (1,)), ((), ())), preferred_element_type=F32)


def _ones_for_k(kv32):
    lane = lax.broadcasted_iota(jnp.int32, kv32.shape, 1)
    return jnp.where(lane < HEAD_DIM, 1.0, kv32).astype(BF16)


def _split_pv(o2):
    n = o2.shape[0] // 2
    lo = lax.broadcasted_iota(jnp.int32, (n, LANES), 1) < HEAD_DIM
    pv = jnp.where(lo, pltpu.roll(o2[:n], HEAD_DIM, 1), o2[n:])
    den = jnp.where(lo, o2[:n], pltpu.roll(o2[n:], HEAD_DIM, 1))
    return pv, den


def _attn_a_kernel(q_ref, kv_ref, o_ref, kv1_ref, s_ref, *, tk, sub):
    t = kv_ref.shape[0]
    n_sub = q_ref.shape[0] // sub
    w = 2 * sub

    @pl.when(pl.program_id(2) == 0)
    def _():
        kv1_ref[...] = _ones_for_k(kv_ref[...].astype(F32))

    tops = []
    for h in range(n_sub):
        q2 = _stack_heads(q_ref[h * sub:(h + 1) * sub, :])
        run_max = jnp.full((w, LANES), -jnp.inf, F32)
        for j in range(t // tk):
            sc = _scores(q2, kv_ref[j * tk:(j + 1) * tk, :])
            s_ref[h, :, j * tk:(j + 1) * tk] = sc
            for c in range(tk // LANES):
                run_max = jnp.maximum(run_max, sc[:, c * LANES:(c + 1) * LANES])
        tops.append(jnp.max(run_max, axis=-1, keepdims=True))

    for h in range(n_sub):
        acc = jnp.zeros((w, LANES), F32)
        for j in range(t // tk):
            p = jnp.exp2(s_ref[h, :, j * tk:(j + 1) * tk] - tops[h]).astype(BF16)
            acc = acc + jnp.dot(p, kv1_ref[j * tk:(j + 1) * tk, :], preferred_element_type=F32)
        pv, den = _split_pv(acc)
        o_ref[h * sub:(h + 1) * sub, :] = pv / den


def _attn_a(qkv, tq, tk, sub):
    b, t, _ = qkv.shape
    return pl.pallas_call(
        functools.partial(_attn_a_kernel, tk=tk, sub=sub),
        grid=(b, A_GROUPS, t // tq),
        in_specs=[
            pl.BlockSpec((None, tq, LANES), lambda bb, g, i: (bb, i, Q_BLK["a"] + g)),
            pl.BlockSpec((None, t, LANES), lambda bb, g, i: (bb, 0, KV_BLK["a"] + g)),
        ],
        out_specs=pl.BlockSpec((None, tq, LANES), lambda bb, g, i: (bb, i, g)),
        out_shape=jax.ShapeDtypeStruct((b, t, A_GROUPS * LANES), F32),
        scratch_shapes=[pltpu.VMEM((t, LANES), BF16), pltpu.VMEM((tq // sub, 2 * sub, t), F32)],
        compiler_params=_cparams(3),
        name="attn_global",
    )(qkv, qkv)


def _attn_b_kernel(q_ref, kv_ref, bias_ref, o_ref, kv1_ref, *, unroll):
    rows = q_ref.shape[0] // GRID_W
    n_keys = NA_ROWS * GRID_W
    kv1_ref[...] = _ones_for_k(kv_ref[...].astype(F32))

    def body(it, carry):
        offs, scores = [], []
        for u in range(unroll):
            r = it * unroll + u
            r0 = jnp.clip(r - NA_ROWS // 2, 0, rows - NA_ROWS)
            qoff = pl.multiple_of(r * GRID_W, GRID_W)
            koff = pl.multiple_of(r0 * GRID_W, GRID_W)
            q2 = _stack_heads(q_ref[pl.ds(qoff, GRID_W), :])
            dr = r0 - r + (NA_ROWS - 1)
            bias = jnp.concatenate(
                [jnp.concatenate([bias_ref[h, dr + 2 * j] for j in range(NA_ROWS // 2)], axis=1)
                 for h in range(2)], axis=0)
            scores.append(_scores(q2, kv_ref[pl.ds(koff, n_keys), :]) + bias)
            offs.append((qoff, koff))
        probs = [jnp.exp2(sc - jnp.max(sc, axis=-1, keepdims=True)).astype(BF16) for sc in scores]
        outs = [jnp.dot(p, kv1_ref[pl.ds(koff, n_keys), :], preferred_element_type=F32)
                for p, (_, koff) in zip(probs, offs)]
        for o2, (qoff, _) in zip(outs, offs):
            pv, den = _split_pv(o2)
            o_ref[pl.ds(qoff, GRID_W), :] = pv / den
        return carry

    lax.fori_loop(0, rows // unroll, body, 0)


def _attn_b(qkv, bias, layer, unroll):
    b, t, _ = qkv.shape
    return pl.pallas_call(
        functools.partial(_attn_b_kernel, unroll=unroll),
        grid=(B_GROUPS, b),
        in_specs=[
            pl.BlockSpec((None, t, LANES), lambda g, bb: (bb, 0, Q_BLK["b"] + g)),
            pl.BlockSpec((None, t, LANES), lambda g, bb: (bb, 0, KV_BLK["b"] + g)),
            pl.BlockSpec((None, None, 2, 2 * NA_ROWS - 2, GRID_W, 2 * GRID_W),
                         lambda g, bb: (layer, g, 0, 0, 0, 0)),
        ],
        out_specs=pl.BlockSpec((None, t, LANES), lambda g, bb: (bb, 0, g)),
        out_shape=jax.ShapeDtypeStruct((b, t, B_GROUPS * LANES), F32),
        scratch_shapes=[pltpu.VMEM((t, LANES), BF16)],
        compiler_params=_cparams(2),
        name="attn_neighbourhood",
    )(qkv, qkv, bias)


def _natural_residue(cls, level):
    res = 0
    for d in range(level):
        res = res + ((cls // (STEP ** (level - 1 - d))) % STEP) * (STEP ** d)
    return res


def _attn_c_kernel(q_ref, kv_ref, bias_ref, o_ref, qf_ref, kvf_ref, qg_ref, kvg_ref, qs_ref,
                   kvs_ref, kv1_ref, ma_ref, wa_ref, na_ref, mb_ref, wb_ref, nb_ref, *, tq,
                   unroll):
    t = q_ref.shape[0]
    lo = lax.broadcasted_iota(jnp.int32, (tq, LANES), 1) < HEAD_DIM
    qf_ref[...] = q_ref[...].astype(F32)
    kvf_ref[...] = kv_ref[...].astype(F32)
    f32_src = (qf_ref, kvf_ref)
    states = ((ma_ref, wa_ref, na_ref), (mb_ref, wb_ref, nb_ref))

    for level, dil in enumerate(DILATIONS):
        length = t // dil
        n_tiles = length // tq
        n_keys = min(length, tq + 2 * HALF_WINDOW)
        first, last = level == 0, level == len(DILATIONS) - 1
        prev_len = length * STEP
        st_in, st_out = states[(level + 1) % 2], states[level % 2]

        if first:
            q_src, kv_src = q_ref, kv_ref
            kv1_ref[...] = _ones_for_k(kvf_ref[...])
        else:
            q_src, kv_src = qs_ref, kvs_ref
            f32_dst = None if last else (qg_ref, kvg_ref)

            def gather_class(c, carry, length=length, prev_len=prev_len, f32_src=f32_src,
                             f32_dst=f32_dst):
                parent = c // STEP
                src = pl.ds(parent * prev_len + (c - parent * STEP), length, stride=STEP)
                rows = pl.ds(pl.multiple_of(c * length, LANES), length)
                qd = f32_src[0][src, :]
                kvd = f32_src[1][src, :]
                qs_ref[rows, :] = qd.astype(BF16)
                kvs_ref[rows, :] = kvd.astype(BF16)
                kv1_ref[rows, :] = _ones_for_k(kvd)
                if f32_dst is not None:
                    f32_dst[0][rows, :] = qd
                    f32_dst[1][rows, :] = kvd
                return carry

            lax.fori_loop(0, dil, gather_class, 0)
            f32_src = f32_dst

        def body(it, carry, level=level, dil=dil, length=length, n_tiles=n_tiles, n_keys=n_keys,
                 first=first, last=last, prev_len=prev_len, q_src=q_src, kv_src=kv_src,
                 st_in=st_in, st_out=st_out):
            where, scores = [], []
            for u in range(unroll):
                g = it * unroll + u
                c = g // n_tiles
                i = g - c * n_tiles
                a0 = i * tq
                start = jnp.clip(a0 - HALF_WINDOW, 0, length - n_keys)
                kind = jnp.where(i == 0, 0, jnp.where(i == n_tiles - 1, 2, 1))
                own = pl.ds(pl.multiple_of(g * tq, tq), tq)
                krow = pl.multiple_of(c * length + start, HALF_WINDOW)
                q2 = _stack_heads(q_src[own, :])
                scores.append(_scores(q2, kv_src[pl.ds(krow, n_keys), :])
                              + bias_ref[kind, :, :n_keys])
                parent = c // STEP
                prev = pl.ds(parent * prev_len + STEP * a0 + (c - parent * STEP), tq, stride=STEP)
                out = pl.ds(dil * a0 + _natural_residue(c, level), tq, stride=dil)
                where.append((krow, own, prev, out))
            tops = [jnp.max(sc, axis=-1, keepdims=True) for sc in scores]
            probs = [jnp.exp2(sc - m).astype(BF16) for sc, m in zip(scores, tops)]
            outs = [jnp.dot(p, kv1_ref[pl.ds(krow, n_keys), :], preferred_element_type=F32)
                    for p, (krow, _, _, _) in zip(probs, where)]
            for o2, m, (_, own, prev, out) in zip(outs, tops, where):
                pv, den = _split_pv(o2)
                top = jnp.where(lo, jnp.broadcast_to(m[:tq], (tq, LANES)),
                                jnp.broadcast_to(m[tq:], (tq, LANES)))
                if not first:
                    m_old = st_in[0][prev, :]
                    m_new = jnp.maximum(m_old, top)
                    e_old = jnp.exp2(m_old - m_new)
                    e_new = jnp.exp2(top - m_new)
                    pv = e_old * st_in[2][prev, :] + e_new * pv
                    den = e_old * st_in[1][prev, :] + e_new * den
                    top = m_new
                if last:
                    o_ref[out, :] = pv / den
                else:
                    st_out[0][own, :] = top
                    st_out[1][own, :] = den
                    st_out[2][own, :] = pv
            return carry

        lax.fori_loop(0, (t // tq) // unroll, body, 0)


def _band_bias(tq):
    row = jnp.arange(2 * tq)[:, None] % tq
    col = jnp.arange(tq + 2 * HALF_WINDOW)[None, :]
    delta = jnp.arange(3)[:, None, None] * HALF_WINDOW
    return jnp.where(jnp.abs(delta + row - col) <= HALF_WINDOW, 0.0, NEG_INF).astype(F32)


def _attn_c(qkv, band, tq, unroll):
    b, t, _ = qkv.shape
    f32 = pltpu.VMEM((t, LANES), F32)
    b16 = pltpu.VMEM((t, LANES), BF16)
    return pl.pallas_call(
        functools.partial(_attn_c_kernel, tq=tq, unroll=unroll),
        grid=(C_GROUPS, b),
        in_specs=[
            pl.BlockSpec((None, t, LANES), lambda g, bb: (bb, 0, Q_BLK["c"] + g)),
            pl.BlockSpec((None, t, LANES), lambda g, bb: (bb, 0, KV_BLK["c"] + g)),
            _resident(band.shape, lambda g, bb: (0, 0, 0)),
        ],
        out_specs=pl.BlockSpec((None, t, LANES), lambda g, bb: (bb, 0, g)),
        out_shape=jax.ShapeDtypeStruct((b, t, C_GROUPS * LANES), F32),
        scratch_shapes=[f32] * 4 + [b16] * 3 + [f32] * 6,
        compiler_params=_cparams(2),
        name="attn_dilated",
    )(qkv, qkv, band)


def _tail_kernel(x_ref, oa_ref, ob_ref, oc_ref, og_ref, wo_ref, nf_ref, wgu_ref, wd_ref,
                 out_ref, y_ref, act_ref, *, fchunk):
    a_w = A_GROUPS * LANES
    b_w = B_GROUPS * LANES
    y_ref[:, :a_w] = _rms(oa_ref[...], og_ref[:, :a_w]).astype(BF16)
    y_ref[:, a_w:a_w + b_w] = _rms(ob_ref[...], og_ref[:, a_w:a_w + b_w]).astype(BF16)
    y_ref[:, a_w + b_w:] = _rms(oc_ref[...], og_ref[:, a_w + b_w:]).astype(BF16)

    x1 = x_ref[...] + jnp.dot(y_ref[...], wo_ref[...], preferred_element_type=F32)
    h = _rms(x1, nf_ref[...]).astype(BF16)
    for c in range(FFN_DIM // fchunk):
        gate = jnp.dot(h, wgu_ref[:, c * fchunk:(c + 1) * fchunk], preferred_element_type=F32)
        up = jnp.dot(h, wgu_ref[:, FFN_DIM + c * fchunk:FFN_DIM + (c + 1) * fchunk],
                     preferred_element_type=F32)
        act_ref[:, c * fchunk:(c + 1) * fchunk] = (jax.nn.silu(gate) * up).astype(BF16)
    out_ref[...] = x1 + jnp.dot(act_ref[...], wd_ref[...], preferred_element_type=F32)


def _tail(x, oa, ob, oc, layer, out_gain, w_out, norm_ffn, w_gu, w_down, tm, fchunk):
    b, t, d = x.shape
    n = b * t
    flat = lambda a: a.reshape(n, a.shape[-1])
    row = lambda w: pl.BlockSpec((tm, w), lambda i: (i, 0))
    out = pl.pallas_call(
        functools.partial(_tail_kernel, fchunk=fchunk),
        grid=(n // tm,),
        in_specs=[
            row(d), row(A_GROUPS * LANES), row(B_GROUPS * LANES), row(C_GROUPS * LANES),
            _resident((None, 1, d), lambda i: (layer, 0, 0)),
            _resident((None, d, d), lambda i: (layer, 0, 0)),
            _resident((None, 1, d), lambda i: (layer, 0, 0)),
            _resident((None, d, 2 * FFN_DIM), lambda i: (layer, 0, 0)),
            _resident((None, FFN_DIM, d), lambda i: (layer, 0, 0)),
        ],
        out_specs=row(d),
        out_shape=jax.ShapeDtypeStruct((n, d), F32),
        scratch_shapes=[pltpu.VMEM((tm, d), BF16), pltpu.VMEM((tm, FFN_DIM), BF16)],
        compiler_params=_cparams(1),
        name="out_proj_ffn",
    )(flat(x), flat(oa), flat(ob), flat(oc),
      out_gain, w_out, norm_ffn, w_gu, w_down)
    return out.reshape(b, t, d)


def _rope_tables(pos, dim):
    inv_freq = ROPE_THETA ** (-jnp.arange(0, dim, 2, dtype=F32) / dim)
    ang = pos[:, None] * inv_freq[None, :]
    ang = jnp.concatenate([ang, ang], axis=-1)
    return jnp.cos(ang), jnp.sin(ang)


def _rope_stack(t):
    pos = jnp.arange(t, dtype=jnp.int32)
    half = HEAD_DIM // 2
    cos_r, sin_r = _rope_tables((pos // GRID_W).astype(F32), half)
    cos_c, sin_c = _rope_tables((pos % GRID_W).astype(F32), half)
    cos_1, sin_1 = _rope_tables(pos.astype(F32), HEAD_DIM)
    j = jnp.arange(HEAD_DIM)
    cos_a = jnp.concatenate([cos_r, cos_c], axis=-1)
    sin_a = jnp.concatenate([sin_r, sin_c], axis=-1) * jnp.where(j % half < half // 2, -1.0, 1.0)
    sin_1 = sin_1 * jnp.where(j < half, -1.0, 1.0)
    one, zero = jnp.ones_like(cos_a), jnp.zeros_like(cos_a)
    cat = lambda u, v: jnp.concatenate([u, v], axis=-1)
    return jnp.stack([
        cat(cos_a, cos_a) * SCALE, cat(sin_a, sin_a) * SCALE, cat(cos_a, one), cat(sin_a, zero),
        cat(cos_1, cos_1) * SCALE, cat(sin_1, sin_1) * SCALE, cat(cos_1, one), cat(sin_1, zero),
    ]).astype(F32)


def _neighbourhood_bias(rpb):
    depth = rpb.shape[0]
    qc = jnp.arange(GRID_W)[:, None]
    kc = jnp.arange(GRID_W)[None, :]
    c0 = jnp.clip(qc - NA_COLS // 2, 0, GRID_W - NA_COLS)
    valid = (kc >= c0) & (kc < c0 + NA_COLS)
    dc = jnp.clip(kc - qc + NA_COLS - 1, 0, 2 * NA_COLS - 2)
    band = jnp.where(valid, (rpb * LOG2E)[..., dc], NEG_INF)
    pairs = jnp.concatenate([band[:, :, :-1], band[:, :, 1:]], axis=-1)
    return pairs.reshape(depth, B_GROUPS, 2, 2 * NA_ROWS - 2, GRID_W, 2 * GRID_W).astype(F32)


def _prepare(norm_mix, w_in, q_gain, k_gain, rpb, out_gain, w_out, norm_ffn, w_gate_up, w_down):
    depth = w_in.shape[0]
    q_w = N_HEADS * HEAD_DIM
    kv_w = KV_HEADS * HEAD_DIM
    w16 = w_in.astype(BF16)
    wq = w16[..., :q_w]
    wk = w16[..., q_w:q_w + kv_w].reshape(depth, D_MODEL, KV_HEADS, 1, HEAD_DIM)
    wv = w16[..., q_w + kv_w:].reshape(depth, D_MODEL, KV_HEADS, 1, HEAD_DIM)
    wkv = jnp.concatenate([wk, wv], axis=3).reshape(depth, D_MODEL, 2 * kv_w)
    w_qkv = jnp.concatenate([wq, wkv], axis=-1)

    heads_q = (4, 6, 6)
    heads_kv = (2, 3, 3)
    gq = jnp.concatenate([jnp.tile(q_gain[:, m], (1, heads_q[m])) for m in range(3)], axis=-1)
    ones = jnp.ones((depth, HEAD_DIM), F32)
    gkv = jnp.concatenate(
        [jnp.tile(jnp.concatenate([k_gain[:, m], ones], axis=-1), (1, heads_kv[m]))
         for m in range(3)], axis=-1)
    gains = jnp.concatenate([gq, gkv], axis=-1).reshape(depth, 1, QKV_WIDTH).astype(F32)

    seg = jnp.arange(PAIR) // HEAD_DIM
    head_ones = (seg[:, None] == seg[None, :]).astype(BF16)
    return dict(
        norm_mix=norm_mix.reshape(depth, 1, D_MODEL), w_qkv=w_qkv, gains=gains,
        head_ones=head_ones, out_gain=out_gain.reshape(depth, 1, D_MODEL),
        w_out=w_out.astype(BF16), norm_ffn=norm_ffn.reshape(depth, 1, D_MODEL),
        w_gu=w_gate_up.astype(BF16), w_down=w_down.astype(BF16),
        bias=_neighbourhood_bias(rpb), band=_band_bias(TQ_C))


def _trunk(x, prm, *, tm_qkv=1024, tq_a=512, tk_a=512, sub_a=128, unroll_b=32, unroll_c=16,
           tm_tail=512, fchunk=256):
    b, t, _ = x.shape
    depth = prm["w_qkv"].shape[0]
    rope = _rope_stack(t)
    bias, band, tq_c = prm["bias"], prm["band"], TQ_C
    for layer in range(depth):
        qkv = _qkv_proj(x, layer, prm["norm_mix"], prm["w_qkv"], prm["gains"], rope,
                        prm["head_ones"], tm_qkv)
        oa = _attn_a(qkv, tq_a, tk_a, sub_a)
        ob = _attn_b(qkv, bias, layer, unroll_b)
        oc = _attn_c(qkv, band, tq_c, unroll_c)
        x = _tail(x, oa, ob, oc, layer, prm["out_gain"], prm["w_out"], prm["norm_ffn"],
                  prm["w_gu"], prm["w_down"], tm_tail, fchunk)
    return x


def kernel(x_prompt, x_sample, norm_mix, w_in, q_gain, k_gain, rpb, out_gain, w_out, norm_ffn,
           w_gate_up, w_down):
    prm = _prepare(norm_mix, w_in, q_gain, k_gain, rpb, out_gain, w_out, norm_ffn, w_gate_up,
                   w_down)
    return (_trunk(x_prompt, prm), _trunk(x_sample, prm))
```

```python
import functools

import jax
import jax.numpy as jnp
from jax import lax
from jax.experimental import pallas as pl
from jax.experimental.pallas import tpu as pltpu

F32 = jnp.float32
BF16 = jnp.bfloat16

D_MODEL = 1024
HEAD_DIM = 64
N_HEADS = 16
KV_HEADS = 8
QKV_WIDTH = 2048
FFN_DIM = 2816
GRID_W = 64
NA_ROWS = 8
NA_COLS = 16
STEP = 4
DILATIONS = (1, STEP, STEP * STEP)
HALF_WINDOW = 64
TQ_C = 2 * HALF_WINDOW
ROPE_THETA = 10000.0
RMS_EPS = 1e-6
NEG_INF = -1e30
LOG2E = 1.4426950408889634
SCALE = HEAD_DIM ** -0.5 * LOG2E

LANES = 128
PAIR = 2 * LANES
A_GROUPS, B_GROUPS, C_GROUPS = 2, 3, 3
Q_BLK = {"a": 0, "b": 2, "c": 5}
KV_BLK = {"a": 8, "b": 10, "c": 13}
N_BLKS = QKV_WIDTH // LANES

VMEM_LIMIT = 56 * 1024 * 1024
A_SCORE_BYTES = 32 * 1024 * 1024
A_MAX_QUERIES = 1024


def _cparams(n_axes):
    return pltpu.CompilerParams(
        dimension_semantics=("arbitrary",) * n_axes, vmem_limit_bytes=VMEM_LIMIT)


def _resident(block_shape, index_map):
    return pl.BlockSpec(block_shape, index_map, pipeline_mode=pl.Buffered(1))


def _rms(x, gain):
    ms = jnp.mean(x * x, axis=-1, keepdims=True)
    return x * lax.rsqrt(ms + RMS_EPS) * gain


def _rotate(y, lane, half):
    up = pltpu.roll(y, LANES - half, 1)
    dn = pltpu.roll(y, half, 1)
    return jnp.where((lane % (2 * half)) < half, up, dn)


def _qkv_kernel(x_ref, nrm_ref, w_ref, gain_ref, rope_ref, ones_ref, out_ref, h_ref, *, chunk,
                sub):
    n_sub = x_ref.shape[0] // sub
    for s in range(n_sub):
        rows = slice(s * sub, (s + 1) * sub)
        h_ref[rows, :] = _rms(x_ref[rows, :], nrm_ref[...]).astype(BF16)
    lane = lax.broadcasted_iota(jnp.int32, (sub, LANES), 1)
    is_k = lane < HEAD_DIM
    for c in range(QKV_WIDTH // chunk):
        for s in range(n_sub):
            rows = slice(s * sub, (s + 1) * sub)
            wide = jnp.dot(h_ref[rows, :], w_ref[:, c * chunk:(c + 1) * chunk],
                           preferred_element_type=F32)
            for hb in range(chunk // LANES):
                if hb % 2 == 0:
                    p = wide[:, hb * LANES:hb * LANES + PAIR]
                    ss = jnp.dot((p * p).astype(BF16), ones_ref[...], preferred_element_type=F32)
                    inv = lax.rsqrt(ss * (1.0 / HEAD_DIM) + RMS_EPS)
                blk = c * (chunk // LANES) + hb
                cols = slice((hb % 2) * LANES, (hb % 2 + 1) * LANES)
                is_q = blk < KV_BLK["a"]
                scale = inv[:, cols] if is_q else jnp.where(is_k, inv[:, cols], 1.0)
                y = p[:, cols] * scale * gain_ref[:, blk * LANES:(blk + 1) * LANES]
                if blk < Q_BLK["b"]:
                    tab, half = 0, HEAD_DIM // 4
                elif blk < Q_BLK["c"]:
                    tab, half = None, None
                elif is_q:
                    tab, half = 4, HEAD_DIM // 2
                elif blk < KV_BLK["b"]:
                    tab, half = 2, HEAD_DIM // 4
                elif blk < KV_BLK["c"]:
                    tab, half = None, None
                else:
                    tab, half = 6, HEAD_DIM // 2
                if tab is not None:
                    y = (y * rope_ref[tab, rows, :]
                         + _rotate(y, lane, half) * rope_ref[tab + 1, rows, :])
                elif is_q:
                    y = y * SCALE
                out_ref[rows, blk * LANES:(blk + 1) * LANES] = y.astype(BF16)


def _qkv_proj(x, layer, nrm, w_in, gains, rope, ones, tm):
    b, t, d = x.shape
    return pl.pallas_call(
        functools.partial(_qkv_kernel, chunk=4 * LANES, sub=tm // 2),
        grid=(t // tm, b),
        in_specs=[
            pl.BlockSpec((None, tm, d), lambda i, bb: (bb, i, 0)),
            _resident((None, 1, d), lambda i, bb: (layer, 0, 0)),
            _resident((None, d, QKV_WIDTH), lambda i, bb: (layer, 0, 0)),
            _resident((None, 1, QKV_WIDTH), lambda i, bb: (layer, 0, 0)),
            pl.BlockSpec((8, tm, LANES), lambda i, bb: (0, i, 0)),
            _resident((PAIR, PAIR), lambda i, bb: (0, 0)),
        ],
        out_specs=pl.BlockSpec((None, tm, QKV_WIDTH), lambda i, bb: (bb, i, 0)),
        out_shape=jax.ShapeDtypeStruct((b, t, QKV_WIDTH), BF16),
        scratch_shapes=[pltpu.VMEM((tm, d), BF16)],
        compiler_params=_cparams(2),
        name="qkv_proj",
    )(x, nrm, w_in, gains, rope, ones)


def _stack_heads(q):
    q32 = q.astype(F32)
    lane = lax.broadcasted_iota(jnp.int32, q32.shape, 1)
    lo = lane < HEAD_DIM
    q0 = jnp.where(lo, q32, 0.0)
    q1 = jnp.where(lo, pltpu.roll(q32, HEAD_DIM, 1), 0.0)
    return jnp.concatenate([q0, q1], axis=0).astype(BF16)


def _scores(a, b):
    return lax.dot_general(a, b, (((1,), (1,)), ((), ())), preferred_element_type=F32)


def _ones_for_k(kv32):
    lane = lax.broadcasted_iota(jnp.int32, kv32.shape, 1)
    return jnp.where(lane < HEAD_DIM, 1.0, kv32).astype(BF16)


def _split_pv(o2):
    n = o2.shape[0] // 2
    lo = lax.broadcasted_iota(jnp.int32, (n, LANES), 1) < HEAD_DIM
    pv = jnp.where(lo, pltpu.roll(o2[:n], HEAD_DIM, 1), o2[n:])
    den = jnp.where(lo, o2[:n], pltpu.roll(o2[n:], HEAD_DIM, 1))
    return pv, den


def _attn_a_kernel(q_ref, kv_ref, o_ref, kv1_ref, s_ref, *, tk, sub):
    t = kv_ref.shape[0]
    n_sub = q_ref.shape[0] // sub
    w = 2 * sub

    @pl.when(pl.program_id(2) == 0)
    def _():
        kv1_ref[...] = _ones_for_k(kv_ref[...].astype(F32))

    tops = []
    for h in range(n_sub):
        q2 = _stack_heads(q_ref[h * sub:(h + 1) * sub, :])
        run_max = jnp.full((w, LANES), -jnp.inf, F32)
        for j in range(t // tk):
            sc = _scores(q2, kv_ref[j * tk:(j + 1) * tk, :])
            s_ref[h, :, j * tk:(j + 1) * tk] = sc
            for c in range(tk // LANES):
                run_max = jnp.maximum(run_max, sc[:, c * LANES:(c + 1) * LANES])
        tops.append(jnp.max(run_max, axis=-1, keepdims=True))

    for h in range(n_sub):
        acc = jnp.zeros((w, LANES), F32)
        for j in range(t // tk):
            p = jnp.exp2(s_ref[h, :, j * tk:(j + 1) * tk] - tops[h]).astype(BF16)
            acc = acc + jnp.dot(p, kv1_ref[j * tk:(j + 1) * tk, :], preferred_element_type=F32)
        pv, den = _split_pv(acc)
        o_ref[h * sub:(h + 1) * sub, :] = pv / den


def _attn_a(qkv, tq, tk, sub):
    b, t, _ = qkv.shape
    return pl.pallas_call(
        functools.partial(_attn_a_kernel, tk=tk, sub=sub),
        grid=(b, A_GROUPS, t // tq),
        in_specs=[
            pl.BlockSpec((None, tq, LANES), lambda bb, g, i: (bb, i, Q_BLK["a"] + g)),
            pl.BlockSpec((None, t, LANES), lambda bb, g, i: (bb, 0, KV_BLK["a"] + g)),
        ],
        out_specs=pl.BlockSpec((None, tq, LANES), lambda bb, g, i: (bb, i, g)),
        out_shape=jax.ShapeDtypeStruct((b, t, A_GROUPS * LANES), F32),
        scratch_shapes=[pltpu.VMEM((t, LANES), BF16), pltpu.VMEM((tq // sub, 2 * sub, t), F32)],
        compiler_params=_cparams(3),
        name="attn_global",
    )(qkv, qkv)


def _attn_b_kernel(q_ref, kv_ref, bias_ref, o_ref, kv1_ref, *, unroll):
    rows = q_ref.shape[0] // GRID_W
    n_keys = NA_ROWS * GRID_W
    kv1_ref[...] = _ones_for_k(kv_ref[...].astype(F32))

    def body(it, carry):
        offs, scores = [], []
        for u in range(unroll):
            r = it * unroll + u
            r0 = jnp.clip(r - NA_ROWS // 2, 0, rows - NA_ROWS)
            qoff = pl.multiple_of(r * GRID_W, GRID_W)
            koff = pl.multiple_of(r0 * GRID_W, GRID_W)
            q2 = _stack_heads(q_ref[pl.ds(qoff, GRID_W), :])
            dr = r0 - r + (NA_ROWS - 1)
            bias = jnp.concatenate(
                [jnp.concatenate([bias_ref[h, dr + 2 * j] for j in range(NA_ROWS // 2)], axis=1)
                 for h in range(2)], axis=0)
            scores.append(_scores(q2, kv_ref[pl.ds(koff, n_keys), :]) + bias)
            offs.append((qoff, koff))
        probs = [jnp.exp2(sc - jnp.max(sc, axis=-1, keepdims=True)).astype(BF16) for sc in scores]
        outs = [jnp.dot(p, kv1_ref[pl.ds(koff, n_keys), :], preferred_element_type=F32)
                for p, (_, koff) in zip(probs, offs)]
        for o2, (qoff, _) in zip(outs, offs):
            pv, den = _split_pv(o2)
            o_ref[pl.ds(qoff, GRID_W), :] = pv / den
        return carry

    lax.fori_loop(0, rows // unroll, body, 0)


def _attn_b(qkv, bias, layer, unroll):
    b, t, _ = qkv.shape
    return pl.pallas_call(
        functools.partial(_attn_b_kernel, unroll=unroll),
        grid=(B_GROUPS, b),
        in_specs=[
            pl.BlockSpec((None, t, LANES), lambda g, bb: (bb, 0, Q_BLK["b"] + g)),
            pl.BlockSpec((None, t, LANES), lambda g, bb: (bb, 0, KV_BLK["b"] + g)),
            pl.BlockSpec((None, None, 2, 2 * NA_ROWS - 2, GRID_W, 2 * GRID_W),
                         lambda g, bb: (layer, g, 0, 0, 0, 0)),
        ],
        out_specs=pl.BlockSpec((None, t, LANES), lambda g, bb: (bb, 0, g)),
        out_shape=jax.ShapeDtypeStruct((b, t, B_GROUPS * LANES), F32),
        scratch_shapes=[pltpu.VMEM((t, LANES), BF16)],
        compiler_params=_cparams(2),
        name="attn_neighbourhood",
    )(qkv, qkv, bias)


def _natural_residue(cls, level):
    res = 0
    for d in range(level):
        res = res + ((cls // (STEP ** (level - 1 - d))) % STEP) * (STEP ** d)
    return res


def _attn_c_kernel(q_ref, kv_ref, bias_ref, o_ref, *scratch, tq, unroll):
    for bi in range(q_ref.shape[0]):
        _attn_c_one(q_ref.at[bi], kv_ref.at[bi], bias_ref, o_ref.at[bi], *scratch, tq=tq,
                    unroll=unroll)


def _attn_c_one(q_ref, kv_ref, bias_ref, o_ref, qf_ref, kvf_ref, qg_ref, kvg_ref, qs_ref,
                kvs_ref, kv1_ref, ma_ref, wa_ref, na_ref, mb_ref, wb_ref, nb_ref, *, tq,
                unroll):
    t = q_ref.shape[0]
    lo = lax.broadcasted_iota(jnp.int32, (tq, LANES), 1) < HEAD_DIM
    qf_ref[...] = q_ref[...].astype(F32)
    kvf_ref[...] = kv_ref[...].astype(F32)
    f32_src = (qf_ref, kvf_ref)
    states = ((ma_ref, wa_ref, na_ref), (mb_ref, wb_ref, nb_ref))

    for level, dil in enumerate(DILATIONS):
        length = t // dil
        n_tiles = length // tq
        n_keys = min(length, tq + 2 * HALF_WINDOW)
        first, last = level == 0, level == len(DILATIONS) - 1
        prev_len = length * STEP
        st_in, st_out = states[(level + 1) % 2], states[level % 2]

        if first:
            q_src, kv_src = q_ref, kv_ref
            kv1_ref[...] = _ones_for_k(kvf_ref[...])
        else:
            q_src, kv_src = qs_ref, kvs_ref
            f32_dst = None if last else (qg_ref, kvg_ref)

            def gather_class(c, carry, length=length, prev_len=prev_len, f32_src=f32_src,
                             f32_dst=f32_dst):
                parent = c // STEP
                src = pl.ds(parent * prev_len + (c - parent * STEP), length, stride=STEP)
                rows = pl.ds(pl.multiple_of(c * length, LANES), length)
                qd = f32_src[0][src, :]
                kvd = f32_src[1][src, :]
                qs_ref[rows, :] = qd.astype(BF16)
                kvs_ref[rows, :] = kvd.astype(BF16)
                kv1_ref[rows, :] = _ones_for_k(kvd)
                if f32_dst is not None:
                    f32_dst[0][rows, :] = qd
                    f32_dst[1][rows, :] = kvd
                return carry

            lax.fori_loop(0, dil, gather_class, 0)
            f32_src = f32_dst

        def body(it, carry, level=level, dil=dil, length=length, n_tiles=n_tiles, n_keys=n_keys,
                 first=first, last=last, prev_len=prev_len, q_src=q_src, kv_src=kv_src,
                 st_in=st_in, st_out=st_out):
            where, scores = [], []
            for u in range(unroll):
                g = it * unroll + u
                c = g // n_tiles
                i = g - c * n_tiles
                a0 = i * tq
                start = jnp.clip(a0 - HALF_WINDOW, 0, length - n_keys)
                kind = jnp.where(i == 0, 0, jnp.where(i == n_tiles - 1, 2, 1))
                own = pl.ds(pl.multiple_of(g * tq, tq), tq)
                krow = pl.multiple_of(c * length + start, HALF_WINDOW)
                q2 = _stack_heads(q_src[own, :])
                scores.append(_scores(q2, kv_src[pl.ds(krow, n_keys), :])
                              + bias_ref[kind, :, :n_keys])
                parent = c // STEP
                prev = pl.ds(parent * prev_len + STEP * a0 + (c - parent * STEP), tq, stride=STEP)
                out = pl.ds(dil * a0 + _natural_residue(c, level), tq, stride=dil)
                where.append((krow, own, prev, out))
            tops = [jnp.max(sc, axis=-1, keepdims=True) for sc in scores]
            probs = [jnp.exp2(sc - m).astype(BF16) for sc, m in zip(scores, tops)]
            outs = [jnp.dot(p, kv1_ref[pl.ds(krow, n_keys), :], preferred_element_type=F32)
                    for p, (krow, _, _, _) in zip(probs, where)]
            for o2, m, (_, own, prev, out) in zip(outs, tops, where):
                pv, den = _split_pv(o2)
                top = jnp.where(lo, jnp.broadcast_to(m[:tq], (tq, LANES)),
                                jnp.broadcast_to(m[tq:], (tq, LANES)))
                if not first:
                    m_old = st_in[0][prev, :]
                    m_new = jnp.maximum(m_old, top)
                    e_old = jnp.exp2(m_old - m_new)
                    e_new = jnp.exp2(top - m_new)
                    pv = e_old * st_in[2][prev, :] + e_new * pv
                    den = e_old * st_in[1][prev, :] + e_new * den
                    top = m_new
                if last:
                    o_ref[out, :] = pv / den
                else:
                    st_out[0][own, :] = top
                    st_out[1][own, :] = den
                    st_out[2][own, :] = pv
            return carry

        lax.fori_loop(0, (t // tq) // unroll, body, 0)


def _band_bias(tq):
    row = jnp.arange(2 * tq)[:, None] % tq
    col = jnp.arange(tq + 2 * HALF_WINDOW)[None, :]
    delta = jnp.arange(3)[:, None, None] * HALF_WINDOW
    return jnp.where(jnp.abs(delta + row - col) <= HALF_WINDOW, 0.0, NEG_INF).astype(F32)


def _attn_c(qkv, band, tq, unroll):
    b, t, _ = qkv.shape
    nb = 2 if t <= 2048 else 1
    f32 = pltpu.VMEM((t, LANES), F32)
    b16 = pltpu.VMEM((t, LANES), BF16)
    return pl.pallas_call(
        functools.partial(_attn_c_kernel, tq=tq, unroll=unroll),
        grid=(C_GROUPS, b // nb),
        in_specs=[
            pl.BlockSpec((nb, t, LANES), lambda g, bb: (bb, 0, Q_BLK["c"] + g)),
            pl.BlockSpec((nb, t, LANES), lambda g, bb: (bb, 0, KV_BLK["c"] + g)),
            _resident(band.shape, lambda g, bb: (0, 0, 0)),
        ],
        out_specs=pl.BlockSpec((nb, t, LANES), lambda g, bb: (bb, 0, g)),
        out_shape=jax.ShapeDtypeStruct((b, t, C_GROUPS * LANES), F32),
        scratch_shapes=[f32] * 4 + [b16] * 3 + [f32] * 6,
        compiler_params=_cparams(2),
        name="attn_dilated",
    )(qkv, qkv, band)


def _tail_kernel(x_ref, oa_ref, ob_ref, oc_ref, og_ref, wo_ref, nf_ref, wgu_ref, wd_ref,
                 out_ref, y_ref, act_ref, h_ref, *, fchunk, sub):
    a_w = A_GROUPS * LANES
    b_w = B_GROUPS * LANES
    subs = [slice(s * sub, (s + 1) * sub) for s in range(x_ref.shape[0] // sub)]
    for rows in subs:
        y_ref[rows, :a_w] = _rms(oa_ref[rows, :], og_ref[:, :a_w]).astype(BF16)
        y_ref[rows, a_w:a_w + b_w] = _rms(ob_ref[rows, :], og_ref[:, a_w:a_w + b_w]).astype(BF16)
        y_ref[rows, a_w + b_w:] = _rms(oc_ref[rows, :], og_ref[:, a_w + b_w:]).astype(BF16)
    for rows in subs:
        x1 = x_ref[rows, :] + jnp.dot(y_ref[rows, :], wo_ref[...], preferred_element_type=F32)
        out_ref[rows, :] = x1
        h_ref[rows, :] = _rms(x1, nf_ref[...]).astype(BF16)
    for c in range(FFN_DIM // fchunk):
        for rows in subs:
            h = h_ref[rows, :]
            gate = jnp.dot(h, wgu_ref[:, c * fchunk:(c + 1) * fchunk], preferred_element_type=F32)
            up = jnp.dot(h, wgu_ref[:, FFN_DIM + c * fchunk:FFN_DIM + (c + 1) * fchunk],
                         preferred_element_type=F32)
            act_ref[rows, c * fchunk:(c + 1) * fchunk] = (jax.nn.silu(gate) * up).astype(BF16)
    for rows in subs:
        out_ref[rows, :] = out_ref[rows, :] + jnp.dot(act_ref[rows, :], wd_ref[...],
                                                      preferred_element_type=F32)


def _tail(x, oa, ob, oc, layer, out_gain, w_out, norm_ffn, w_gu, w_down, tm, fchunk):
    b, t, d = x.shape
    n = b * t
    flat = lambda a: a.reshape(n, a.shape[-1])
    row = lambda w: pl.BlockSpec((tm, w), lambda i: (i, 0))
    out = pl.pallas_call(
        functools.partial(_tail_kernel, fchunk=fchunk, sub=tm // 2),
        grid=(n // tm,),
        in_specs=[
            row(d), row(A_GROUPS * LANES), row(B_GROUPS * LANES), row(C_GROUPS * LANES),
            _resident((None, 1, d), lambda i: (layer, 0, 0)),
            _resident((None, d, d), lambda i: (layer, 0, 0)),
            _resident((None, 1, d), lambda i: (layer, 0, 0)),
            _resident((None, d, 2 * FFN_DIM), lambda i: (layer, 0, 0)),
            _resident((None, FFN_DIM, d), lambda i: (layer, 0, 0)),
        ],
        out_specs=row(d),
        out_shape=jax.ShapeDtypeStruct((n, d), F32),
        scratch_shapes=[pltpu.VMEM((tm, d), BF16), pltpu.VMEM((tm, FFN_DIM), BF16),
                        pltpu.VMEM((tm, d), BF16)],
        compiler_params=_cparams(1),
        name="out_proj_ffn",
    )(flat(x), flat(oa), flat(ob), flat(oc),
      out_gain, w_out, norm_ffn, w_gu, w_down)
    return out.reshape(b, t, d)


def _rope_tables(pos, dim):
    inv_freq = ROPE_THETA ** (-jnp.arange(0, dim, 2, dtype=F32) / dim)
    ang = pos[:, None] * inv_freq[None, :]
    ang = jnp.concatenate([ang, ang], axis=-1)
    return jnp.cos(ang), jnp.sin(ang)


def _rope_stack(t):
    pos = jnp.arange(t, dtype=jnp.int32)
    half = HEAD_DIM // 2
    cos_r, sin_r = _rope_tables((pos // GRID_W).astype(F32), half)
    cos_c, sin_c = _rope_tables((pos % GRID_W).astype(F32), half)
    cos_1, sin_1 = _rope_tables(pos.astype(F32), HEAD_DIM)
    j = jnp.arange(HEAD_DIM)
    cos_a = jnp.concatenate([cos_r, cos_c], axis=-1)
    sin_a = jnp.concatenate([sin_r, sin_c], axis=-1) * jnp.where(j % half < half // 2, -1.0, 1.0)
    sin_1 = sin_1 * jnp.where(j < half, -1.0, 1.0)
    one, zero = jnp.ones_like(cos_a), jnp.zeros_like(cos_a)
    cat = lambda u, v: jnp.concatenate([u, v], axis=-1)
    return jnp.stack([
        cat(cos_a, cos_a) * SCALE, cat(sin_a, sin_a) * SCALE, cat(cos_a, one), cat(sin_a, zero),
        cat(cos_1, cos_1) * SCALE, cat(sin_1, sin_1) * SCALE, cat(cos_1, one), cat(sin_1, zero),
    ]).astype(F32)


def _neighbourhood_bias(rpb):
    depth = rpb.shape[0]
    qc = jnp.arange(GRID_W)[:, None]
    kc = jnp.arange(GRID_W)[None, :]
    c0 = jnp.clip(qc - NA_COLS // 2, 0, GRID_W - NA_COLS)
    valid = (kc >= c0) & (kc < c0 + NA_COLS)
    dc = jnp.clip(kc - qc + NA_COLS - 1, 0, 2 * NA_COLS - 2)
    band = jnp.where(valid, (rpb * LOG2E)[..., dc], NEG_INF)
    pairs = jnp.concatenate([band[:, :, :-1], band[:, :, 1:]], axis=-1)
    return pairs.reshape(depth, B_GROUPS, 2, 2 * NA_ROWS - 2, GRID_W, 2 * GRID_W).astype(F32)


def _prepare(norm_mix, w_in, q_gain, k_gain, rpb, out_gain, w_out, norm_ffn, w_gate_up, w_down):
    depth = w_in.shape[0]
    q_w = N_HEADS * HEAD_DIM
    kv_w = KV_HEADS * HEAD_DIM
    w16 = w_in.astype(BF16)
    wq = w16[..., :q_w]
    wk = w16[..., q_w:q_w + kv_w].reshape(depth, D_MODEL, KV_HEADS, 1, HEAD_DIM)
    wv = w16[..., q_w + kv_w:].reshape(depth, D_MODEL, KV_HEADS, 1, HEAD_DIM)
    wkv = jnp.concatenate([wk, wv], axis=3).reshape(depth, D_MODEL, 2 * kv_w)
    w_qkv = jnp.concatenate([wq, wkv], axis=-1)

    heads_q = (4, 6, 6)
    heads_kv = (2, 3, 3)
    gq = jnp.concatenate([jnp.tile(q_gain[:, m], (1, heads_q[m])) for m in range(3)], axis=-1)
    ones = jnp.ones((depth, HEAD_DIM), F32)
    gkv = jnp.concatenate(
        [jnp.tile(jnp.concatenate([k_gain[:, m], ones], axis=-1), (1, heads_kv[m]))
         for m in range(3)], axis=-1)
    gains = jnp.concatenate([gq, gkv], axis=-1).reshape(depth, 1, QKV_WIDTH).astype(F32)

    seg = jnp.arange(PAIR) // HEAD_DIM
    head_ones = (seg[:, None] == seg[None, :]).astype(BF16)
    return dict(
        norm_mix=norm_mix.reshape(depth, 1, D_MODEL), w_qkv=w_qkv, gains=gains,
        head_ones=head_ones, out_gain=out_gain.reshape(depth, 1, D_MODEL),
        w_out=w_out.astype(BF16), norm_ffn=norm_ffn.reshape(depth, 1, D_MODEL),
        w_gu=w_gate_up.astype(BF16), w_down=w_down.astype(BF16),
        bias=_neighbourhood_bias(rpb), band=_band_bias(TQ_C))


def _trunk(x, prm, *, tm_qkv=1024, tk_a=512, sub_a=128, unroll_b=32, unroll_c=16, tm_tail=512,
           fchunk=256):
    b, t, _ = x.shape
    tq_a = min(A_MAX_QUERIES, A_SCORE_BYTES // (2 * 4 * t))
    depth = prm["w_qkv"].shape[0]
    rope = _rope_stack(t)
    bias, band, tq_c = prm["bias"], prm["band"], TQ_C
    for layer in range(depth):
        qkv = _qkv_proj(x, layer, prm["norm_mix"], prm["w_qkv"], prm["gains"], rope,
                        prm["head_ones"], tm_qkv)
        oa = _attn_a(qkv, tq_a, tk_a, sub_a)
        ob = _attn_b(qkv, bias, layer, unroll_b)
        oc = _attn_c(qkv, band, tq_c, unroll_c)
        x = _tail(x, oa, ob, oc, layer, prm["out_gain"], prm["w_out"], prm["norm_ffn"],
                  prm["w_gu"], prm["w_down"], tm_tail, fchunk)
    return x


def kernel(x_prompt, x_sample, norm_mix, w_in, q_gain, k_gain, rpb, out_gain, w_out, norm_ffn,
           w_gate_up, w_down):
    prm = _prepare(norm_mix, w_in, q_gain, k_gain, rpb, out_gain, w_out, norm_ffn, w_gate_up,
                   w_down)
    return (_trunk(x_prompt, prm), _trunk(x_sample, prm))
```

```python
import functools

import jax
import jax.numpy as jnp
from jax import lax
from jax.experimental import pallas as pl
from jax.experimental.pallas import tpu as pltpu

F32 = jnp.float32
BF16 = jnp.bfloat16

D_MODEL = 1024
HEAD_DIM = 64
N_HEADS = 16
KV_HEADS = 8
QKV_WIDTH = 2048
FFN_DIM = 2816
GRID_W = 64
NA_ROWS = 8
NA_COLS = 16
STEP = 4
DILATIONS = (1, STEP, STEP * STEP)
HALF_WINDOW = 64
TQ_C = 2 * HALF_WINDOW
ROPE_THETA = 10000.0
RMS_EPS = 1e-6
NEG_INF = -1e30
LOG2E = 1.4426950408889634
SCALE = HEAD_DIM ** -0.5 * LOG2E

LANES = 128
PAIR = 2 * LANES
A_GROUPS, B_GROUPS, C_GROUPS = 2, 3, 3
B_ROW_SHARE = (12, 10, 10)
Q_BLK = {"a": 0, "b": 2, "c": 5}
KV_BLK = {"a": 8, "b": 10, "c": 13}
N_BLKS = QKV_WIDTH // LANES

VMEM_LIMIT = 56 * 1024 * 1024
A_SCORE_BYTES = 32 * 1024 * 1024
A_MAX_QUERIES = 1024


def _cparams(n_axes):
    return pltpu.CompilerParams(
        dimension_semantics=("arbitrary",) * n_axes, vmem_limit_bytes=VMEM_LIMIT)


def _resident(block_shape, index_map):
    return pl.BlockSpec(block_shape, index_map, pipeline_mode=pl.Buffered(1))


def _rms(x, gain):
    ms = jnp.mean(x * x, axis=-1, keepdims=True)
    return x * lax.rsqrt(ms + RMS_EPS) * gain


def _rotate(y, lane, half):
    up = pltpu.roll(y, LANES - half, 1)
    dn = pltpu.roll(y, half, 1)
    return jnp.where((lane % (2 * half)) < half, up, dn)


def _qkv_kernel(x_ref, nrm_ref, w_ref, gain_ref, rope_ref, ones_ref, out_ref, h_ref, *, chunk,
                sub):
    n_sub = x_ref.shape[0] // sub
    for s in range(n_sub):
        rows = slice(s * sub, (s + 1) * sub)
        h_ref[rows, :] = _rms(x_ref[rows, :], nrm_ref[...]).astype(BF16)
    lane = lax.broadcasted_iota(jnp.int32, (sub, LANES), 1)
    is_k = lane < HEAD_DIM
    for c in range(QKV_WIDTH // chunk):
        for s in range(n_sub):
            rows = slice(s * sub, (s + 1) * sub)
            wide = jnp.dot(h_ref[rows, :], w_ref[:, c * chunk:(c + 1) * chunk],
                           preferred_element_type=F32)
            for hb in range(chunk // LANES):
                if hb % 2 == 0:
                    p = wide[:, hb * LANES:hb * LANES + PAIR]
                    ss = jnp.dot((p * p).astype(BF16), ones_ref[...], preferred_element_type=F32)
                    inv = lax.rsqrt(ss * (1.0 / HEAD_DIM) + RMS_EPS)
                blk = c * (chunk // LANES) + hb
                cols = slice((hb % 2) * LANES, (hb % 2 + 1) * LANES)
                is_q = blk < KV_BLK["a"]
                scale = inv[:, cols] if is_q else jnp.where(is_k, inv[:, cols], 1.0)
                y = p[:, cols] * scale * gain_ref[:, blk * LANES:(blk + 1) * LANES]
                if blk < Q_BLK["b"]:
                    tab, half = 0, HEAD_DIM // 4
                elif blk < Q_BLK["c"]:
                    tab, half = None, None
                elif is_q:
                    tab, half = 4, HEAD_DIM // 2
                elif blk < KV_BLK["b"]:
                    tab, half = 2, HEAD_DIM // 4
                elif blk < KV_BLK["c"]:
                    tab, half = None, None
                else:
                    tab, half = 6, HEAD_DIM // 2
                if tab is not None:
                    y = (y * rope_ref[tab, rows, :]
                         + _rotate(y, lane, half) * rope_ref[tab + 1, rows, :])
                elif is_q:
                    y = y * SCALE
                out_ref[rows, blk * LANES:(blk + 1) * LANES] = y.astype(BF16)


def _qkv_proj(x, layer, nrm, w_in, gains, rope, ones, tm):
    b, t, d = x.shape
    return pl.pallas_call(
        functools.partial(_qkv_kernel, chunk=4 * LANES, sub=tm // 2),
        grid=(t // tm, b),
        in_specs=[
            pl.BlockSpec((None, tm, d), lambda i, bb: (bb, i, 0)),
            _resident((None, 1, d), lambda i, bb: (layer, 0, 0)),
            _resident((None, d, QKV_WIDTH), lambda i, bb: (layer, 0, 0)),
            _resident((None, 1, QKV_WIDTH), lambda i, bb: (layer, 0, 0)),
            pl.BlockSpec((8, tm, LANES), lambda i, bb: (0, i, 0)),
            _resident((PAIR, PAIR), lambda i, bb: (0, 0)),
        ],
        out_specs=pl.BlockSpec((None, tm, QKV_WIDTH), lambda i, bb: (bb, i, 0)),
        out_shape=jax.ShapeDtypeStruct((b, t, QKV_WIDTH), BF16),
        scratch_shapes=[pltpu.VMEM((tm, d), BF16)],
        compiler_params=_cparams(2),
        name="qkv_proj",
    )(x, nrm, w_in, gains, rope, ones)


def _stack_heads(q):
    q32 = q.astype(F32)
    lane = lax.broadcasted_iota(jnp.int32, q32.shape, 1)
    lo = lane < HEAD_DIM
    q0 = jnp.where(lo, q32, 0.0)
    q1 = jnp.where(lo, pltpu.roll(q32, HEAD_DIM, 1), 0.0)
    return jnp.concatenate([q0, q1], axis=0).astype(BF16)


def _scores(a, b):
    return lax.dot_general(a, b, (((1,), (1,)), ((), ())), preferred_element_type=F32)


def _ones_for_k(kv32):
    lane = lax.broadcasted_iota(jnp.int32, kv32.shape, 1)
    return jnp.where(lane < HEAD_DIM, 1.0, kv32).astype(BF16)


def _split_pv(o2):
    n = o2.shape[0] // 2
    lo = lax.broadcasted_iota(jnp.int32, (n, LANES), 1) < HEAD_DIM
    pv = jnp.where(lo, pltpu.roll(o2[:n], HEAD_DIM, 1), o2[n:])
    den = jnp.where(lo, o2[:n], pltpu.roll(o2[n:], HEAD_DIM, 1))
    return pv, den


def _attn_a_kernel(q_ref, kv_ref, o_ref, kv1_ref, s_ref, *, tk, sub):
    t = kv_ref.shape[0]
    n_sub = q_ref.shape[0] // sub
    w = 2 * sub

    @pl.when(pl.program_id(2) == 0)
    def _():
        kv1_ref[...] = _ones_for_k(kv_ref[...].astype(F32))

    tops = []
    for h in range(n_sub):
        q2 = _stack_heads(q_ref[h * sub:(h + 1) * sub, :])
        run_max = jnp.full((w, LANES), -jnp.inf, F32)
        for j in range(t // tk):
            sc = _scores(q2, kv_ref[j * tk:(j + 1) * tk, :])
            s_ref[h, :, j * tk:(j + 1) * tk] = sc
            for c in range(tk // LANES):
                run_max = jnp.maximum(run_max, sc[:, c * LANES:(c + 1) * LANES])
        tops.append(jnp.max(run_max, axis=-1, keepdims=True))

    for h in range(n_sub):
        acc = jnp.zeros((w, LANES), F32)
        for j in range(t // tk):
            p = jnp.exp2(s_ref[h, :, j * tk:(j + 1) * tk] - tops[h]).astype(BF16)
            acc = acc + jnp.dot(p, kv1_ref[j * tk:(j + 1) * tk, :], preferred_element_type=F32)
        pv, den = _split_pv(acc)
        o_ref[h * sub:(h + 1) * sub, :] = pv / den


def _attn_a(qkv, tq, tk, sub):
    b, t, _ = qkv.shape
    return pl.pallas_call(
        functools.partial(_attn_a_kernel, tk=tk, sub=sub),
        grid=(b, A_GROUPS, t // tq),
        in_specs=[
            pl.BlockSpec((None, tq, LANES), lambda bb, g, i: (bb, i, Q_BLK["a"] + g)),
            pl.BlockSpec((None, t, LANES), lambda bb, g, i: (bb, 0, KV_BLK["a"] + g)),
        ],
        out_specs=pl.BlockSpec((None, tq, LANES), lambda bb, g, i: (bb, i, g)),
        out_shape=jax.ShapeDtypeStruct((b, t, A_GROUPS * LANES), F32),
        scratch_shapes=[pltpu.VMEM((t, LANES), BF16), pltpu.VMEM((tq // sub, 2 * sub, t), F32)],
        compiler_params=_cparams(3),
        name="attn_global",
    )(qkv, qkv)


def _b_row_scores(r, rows, q_ref, kv_ref, bias_ref):
    r0 = jnp.clip(r - NA_ROWS // 2, 0, rows - NA_ROWS)
    qoff = pl.multiple_of(r * GRID_W, GRID_W)
    koff = pl.multiple_of(r0 * GRID_W, GRID_W)
    q2 = _stack_heads(q_ref[pl.ds(qoff, GRID_W), :])
    dr = r0 - r + (NA_ROWS - 1)
    bias = jnp.concatenate(
        [jnp.concatenate([bias_ref[h, dr + 2 * j] for j in range(NA_ROWS // 2)], axis=1)
         for h in range(2)], axis=0)
    return _scores(q2, kv_ref[pl.ds(koff, NA_ROWS * GRID_W), :]) + bias, qoff, koff


def _natural_residue(cls, level):
    res = 0
    for d in range(level):
        res = res + ((cls // (STEP ** (level - 1 - d))) % STEP) * (STEP ** d)
    return res


def _attn_bc_kernel(qb_ref, kvb_ref, biasb_ref, q_ref, kv_ref, bias_ref, ob_ref, o_ref, kv1b_ref,
                    qf_ref, kvf_ref, qg_ref, kvg_ref, qs_ref, kvs_ref, kv1_ref, ma_ref, wa_ref,
                    na_ref, mb_ref, wb_ref, nb_ref, *, tq, unroll):
    t = q_ref.shape[0]
    grid_rows = t // GRID_W
    kv1b_ref[...] = _ones_for_k(kvb_ref[...].astype(F32))
    lo = lax.broadcasted_iota(jnp.int32, (tq, LANES), 1) < HEAD_DIM
    qf_ref[...] = q_ref[...].astype(F32)
    kvf_ref[...] = kv_ref[...].astype(F32)
    f32_src = (qf_ref, kvf_ref)
    states = ((ma_ref, wa_ref, na_ref), (mb_ref, wb_ref, nb_ref))

    for level, dil in enumerate(DILATIONS):
        length = t // dil
        n_tiles = length // tq
        n_keys = min(length, tq + 2 * HALF_WINDOW)
        first, last = level == 0, level == len(DILATIONS) - 1
        prev_len = length * STEP
        st_in, st_out = states[(level + 1) % 2], states[level % 2]
        b_base = grid_rows * sum(B_ROW_SHARE[:level]) // sum(B_ROW_SHARE)
        b_rows = grid_rows * B_ROW_SHARE[level] // sum(B_ROW_SHARE) // ((t // tq) // unroll)

        if first:
            q_src, kv_src = q_ref, kv_ref
            kv1_ref[...] = _ones_for_k(kvf_ref[...])
        else:
            q_src, kv_src = qs_ref, kvs_ref
            f32_dst = None if last else (qg_ref, kvg_ref)

            def gather_class(c, carry, length=length, prev_len=prev_len, f32_src=f32_src,
                             f32_dst=f32_dst):
                parent = c // STEP
                src = pl.ds(parent * prev_len + (c - parent * STEP), length, stride=STEP)
                rows = pl.ds(pl.multiple_of(c * length, LANES), length)
                qd = f32_src[0][src, :]
                kvd = f32_src[1][src, :]
                qs_ref[rows, :] = qd.astype(BF16)
                kvs_ref[rows, :] = kvd.astype(BF16)
                kv1_ref[rows, :] = _ones_for_k(kvd)
                if f32_dst is not None:
                    f32_dst[0][rows, :] = qd
                    f32_dst[1][rows, :] = kvd
                return carry

            lax.fori_loop(0, dil, gather_class, 0)
            f32_src = f32_dst

        def body(it, carry, level=level, dil=dil, length=length, n_tiles=n_tiles, n_keys=n_keys,
                 first=first, last=last, prev_len=prev_len, q_src=q_src, kv_src=kv_src,
                 st_in=st_in, st_out=st_out, b_base=b_base, b_rows=b_rows):
            where, scores = [], []
            for u in range(unroll):
                g = it * unroll + u
                c = g // n_tiles
                i = g - c * n_tiles
                a0 = i * tq
                start = jnp.clip(a0 - HALF_WINDOW, 0, length - n_keys)
                kind = jnp.where(i == 0, 0, jnp.where(i == n_tiles - 1, 2, 1))
                own = pl.ds(pl.multiple_of(g * tq, tq), tq)
                krow = pl.multiple_of(c * length + start, HALF_WINDOW)
                q2 = _stack_heads(q_src[own, :])
                scores.append(_scores(q2, kv_src[pl.ds(krow, n_keys), :])
                              + bias_ref[kind, :, :n_keys])
                parent = c // STEP
                prev = pl.ds(parent * prev_len + STEP * a0 + (c - parent * STEP), tq, stride=STEP)
                out = pl.ds(dil * a0 + _natural_residue(c, level), tq, stride=dil)
                where.append((krow, own, prev, out))
            b_items = [_b_row_scores(b_base + it * b_rows + u, grid_rows, qb_ref, kvb_ref, biasb_ref)
                       for u in range(b_rows)]
            tops = [jnp.max(sc, axis=-1, keepdims=True) for sc in scores]
            probs = [jnp.exp2(sc - m).astype(BF16) for sc, m in zip(scores, tops)]
            b_probs = [jnp.exp2(sc - jnp.max(sc, axis=-1, keepdims=True)).astype(BF16)
                       for sc, _, _ in b_items]
            outs = [jnp.dot(p, kv1_ref[pl.ds(krow, n_keys), :], preferred_element_type=F32)
                    for p, (krow, _, _, _) in zip(probs, where)]
            b_outs = [jnp.dot(p, kv1b_ref[pl.ds(koff, NA_ROWS * GRID_W), :],
                              preferred_element_type=F32)
                      for p, (_, _, koff) in zip(b_probs, b_items)]
            for o2, m, (_, own, prev, out) in zip(outs, tops, where):
                pv, den = _split_pv(o2)
                top = jnp.where(lo, jnp.broadcast_to(m[:tq], (tq, LANES)),
                                jnp.broadcast_to(m[tq:], (tq, LANES)))
                if not first:
                    m_old = st_in[0][prev, :]
                    m_new = jnp.maximum(m_old, top)
                    e_old = jnp.exp2(m_old - m_new)
                    e_new = jnp.exp2(top - m_new)
                    pv = e_old * st_in[2][prev, :] + e_new * pv
                    den = e_old * st_in[1][prev, :] + e_new * den
                    top = m_new
                if last:
                    o_ref[out, :] = pv / den
                else:
                    st_out[0][own, :] = top
                    st_out[1][own, :] = den
                    st_out[2][own, :] = pv
            for o2, (_, qoff, _) in zip(b_outs, b_items):
                pv, den = _split_pv(o2)
                ob_ref[pl.ds(qoff, GRID_W), :] = pv / den
            return carry

        lax.fori_loop(0, (t // tq) // unroll, body, 0)


def _band_bias(tq):
    row = jnp.arange(2 * tq)[:, None] % tq
    col = jnp.arange(tq + 2 * HALF_WINDOW)[None, :]
    delta = jnp.arange(3)[:, None, None] * HALF_WINDOW
    return jnp.where(jnp.abs(delta + row - col) <= HALF_WINDOW, 0.0, NEG_INF).astype(F32)


def _attn_bc(qkv, bias, band, layer, tq, unroll):
    b, t, _ = qkv.shape
    f32 = pltpu.VMEM((t, LANES), F32)
    b16 = pltpu.VMEM((t, LANES), BF16)
    seq = lambda first_blk: pl.BlockSpec((None, t, LANES), lambda g, bb: (bb, 0, first_blk + g))
    out = pl.BlockSpec((None, t, LANES), lambda g, bb: (bb, 0, g))
    return pl.pallas_call(
        functools.partial(_attn_bc_kernel, tq=tq, unroll=unroll),
        grid=(B_GROUPS, b),
        in_specs=[
            seq(Q_BLK["b"]), seq(KV_BLK["b"]),
            pl.BlockSpec((None, None, 2, 2 * NA_ROWS - 2, GRID_W, 2 * GRID_W),
                         lambda g, bb: (layer, g, 0, 0, 0, 0)),
            seq(Q_BLK["c"]), seq(KV_BLK["c"]),
            _resident(band.shape, lambda g, bb: (0, 0, 0)),
        ],
        out_specs=[out, out],
        out_shape=[jax.ShapeDtypeStruct((b, t, B_GROUPS * LANES), F32),
                   jax.ShapeDtypeStruct((b, t, C_GROUPS * LANES), F32)],
        scratch_shapes=[b16] + [f32] * 4 + [b16] * 3 + [f32] * 6,
        compiler_params=_cparams(2),
        name="attn_local",
    )(qkv, qkv, bias, qkv, qkv, band)


def _tail_kernel(x_ref, oa_ref, ob_ref, oc_ref, og_ref, wo_ref, nf_ref, wgu_ref, wd_ref,
                 out_ref, y_ref, act_ref, h_ref, *, fchunk, sub):
    a_w = A_GROUPS * LANES
    b_w = B_GROUPS * LANES
    subs = [slice(s * sub, (s + 1) * sub) for s in range(x_ref.shape[0] // sub)]
    for rows in subs:
        y_ref[rows, :a_w] = _rms(oa_ref[rows, :], og_ref[:, :a_w]).astype(BF16)
        y_ref[rows, a_w:a_w + b_w] = _rms(ob_ref[rows, :], og_ref[:, a_w:a_w + b_w]).astype(BF16)
        y_ref[rows, a_w + b_w:] = _rms(oc_ref[rows, :], og_ref[:, a_w + b_w:]).astype(BF16)
    for rows in subs:
        x1 = x_ref[rows, :] + jnp.dot(y_ref[rows, :], wo_ref[...], preferred_element_type=F32)
        out_ref[rows, :] = x1
        h_ref[rows, :] = _rms(x1, nf_ref[...]).astype(BF16)
    for c in range(FFN_DIM // fchunk):
        for rows in subs:
            h = h_ref[rows, :]
            gate = jnp.dot(h, wgu_ref[:, c * fchunk:(c + 1) * fchunk], preferred_element_type=F32)
            up = jnp.dot(h, wgu_ref[:, FFN_DIM + c * fchunk:FFN_DIM + (c + 1) * fchunk],
                         preferred_element_type=F32)
            act_ref[rows, c * fchunk:(c + 1) * fchunk] = (jax.nn.silu(gate) * up).astype(BF16)
    for rows in subs:
        out_ref[rows, :] = out_ref[rows, :] + jnp.dot(act_ref[rows, :], wd_ref[...],
                                                      preferred_element_type=F32)


def _tail(x, oa, ob, oc, layer, out_gain, w_out, norm_ffn, w_gu, w_down, tm, fchunk):
    b, t, d = x.shape
    n = b * t
    flat = lambda a: a.reshape(n, a.shape[-1])
    row = lambda w: pl.BlockSpec((tm, w), lambda i: (i, 0))
    out = pl.pallas_call(
        functools.partial(_tail_kernel, fchunk=fchunk, sub=tm // 2),
        grid=(n // tm,),
        in_specs=[
            row(d), row(A_GROUPS * LANES), row(B_GROUPS * LANES), row(C_GROUPS * LANES),
            _resident((None, 1, d), lambda i: (layer, 0, 0)),
            _resident((None, d, d), lambda i: (layer, 0, 0)),
            _resident((None, 1, d), lambda i: (layer, 0, 0)),
            _resident((None, d, 2 * FFN_DIM), lambda i: (layer, 0, 0)),
            _resident((None, FFN_DIM, d), lambda i: (layer, 0, 0)),
        ],
        out_specs=row(d),
        out_shape=jax.ShapeDtypeStruct((n, d), F32),
        scratch_shapes=[pltpu.VMEM((tm, d), BF16), pltpu.VMEM((tm, FFN_DIM), BF16),
                        pltpu.VMEM((tm, d), BF16)],
        compiler_params=_cparams(1),
        name="out_proj_ffn",
    )(flat(x), flat(oa), flat(ob), flat(oc),
      out_gain, w_out, norm_ffn, w_gu, w_down)
    return out.reshape(b, t, d)


def _rope_tables(pos, dim):
    inv_freq = ROPE_THETA ** (-jnp.arange(0, dim, 2, dtype=F32) / dim)
    ang = pos[:, None] * inv_freq[None, :]
    ang = jnp.concatenate([ang, ang], axis=-1)
    return jnp.cos(ang), jnp.sin(ang)


def _rope_stack(t):
    pos = jnp.arange(t, dtype=jnp.int32)
    half = HEAD_DIM // 2
    cos_r, sin_r = _rope_tables((pos // GRID_W).astype(F32), half)
    cos_c, sin_c = _rope_tables((pos % GRID_W).astype(F32), half)
    cos_1, sin_1 = _rope_tables(pos.astype(F32), HEAD_DIM)
    j = jnp.arange(HEAD_DIM)
    cos_a = jnp.concatenate([cos_r, cos_c], axis=-1)
    sin_a = jnp.concatenate([sin_r, sin_c], axis=-1) * jnp.where(j % half < half // 2, -1.0, 1.0)
    sin_1 = sin_1 * jnp.where(j < half, -1.0, 1.0)
    one, zero = jnp.ones_like(cos_a), jnp.zeros_like(cos_a)
    cat = lambda u, v: jnp.concatenate([u, v], axis=-1)
    return jnp.stack([
        cat(cos_a, cos_a) * SCALE, cat(sin_a, sin_a) * SCALE, cat(cos_a, one), cat(sin_a, zero),
        cat(cos_1, cos_1) * SCALE, cat(sin_1, sin_1) * SCALE, cat(cos_1, one), cat(sin_1, zero),
    ]).astype(F32)


def _neighbourhood_bias(rpb):
    depth = rpb.shape[0]
    qc = jnp.arange(GRID_W)[:, None]
    kc = jnp.arange(GRID_W)[None, :]
    c0 = jnp.clip(qc - NA_COLS // 2, 0, GRID_W - NA_COLS)
    valid = (kc >= c0) & (kc < c0 + NA_COLS)
    dc = jnp.clip(kc - qc + NA_COLS - 1, 0, 2 * NA_COLS - 2)
    band = jnp.where(valid, (rpb * LOG2E)[..., dc], NEG_INF)
    pairs = jnp.concatenate([band[:, :, :-1], band[:, :, 1:]], axis=-1)
    return pairs.reshape(depth, B_GROUPS, 2, 2 * NA_ROWS - 2, GRID_W, 2 * GRID_W).astype(F32)


def _prepare(norm_mix, w_in, q_gain, k_gain, rpb, out_gain, w_out, norm_ffn, w_gate_up, w_down):
    depth = w_in.shape[0]
    q_w = N_HEADS * HEAD_DIM
    kv_w = KV_HEADS * HEAD_DIM
    w16 = w_in.astype(BF16)
    wq = w16[..., :q_w]
    wk = w16[..., q_w:q_w + kv_w].reshape(depth, D_MODEL, KV_HEADS, 1, HEAD_DIM)
    wv = w16[..., q_w + kv_w:].reshape(depth, D_MODEL, KV_HEADS, 1, HEAD_DIM)
    wkv = jnp.concatenate([wk, wv], axis=3).reshape(depth, D_MODEL, 2 * kv_w)
    w_qkv = jnp.concatenate([wq, wkv], axis=-1)

    heads_q = (4, 6, 6)
    heads_kv = (2, 3, 3)
    gq = jnp.concatenate([jnp.tile(q_gain[:, m], (1, heads_q[m])) for m in range(3)], axis=-1)
    ones = jnp.ones((depth, HEAD_DIM), F32)
    gkv = jnp.concatenate(
        [jnp.tile(jnp.concatenate([k_gain[:, m], ones], axis=-1), (1, heads_kv[m]))
         for m in range(3)], axis=-1)
    gains = jnp.concatenate([gq, gkv], axis=-1).reshape(depth, 1, QKV_WIDTH).astype(F32)

    seg = jnp.arange(PAIR) // HEAD_DIM
    head_ones = (seg[:, None] == seg[None, :]).astype(BF16)
    return dict(
        norm_mix=norm_mix.reshape(depth, 1, D_MODEL), w_qkv=w_qkv, gains=gains,
        head_ones=head_ones, out_gain=out_gain.reshape(depth, 1, D_MODEL),
        w_out=w_out.astype(BF16), norm_ffn=norm_ffn.reshape(depth, 1, D_MODEL),
        w_gu=w_gate_up.astype(BF16), w_down=w_down.astype(BF16),
        bias=_neighbourhood_bias(rpb), band=_band_bias(TQ_C))


def _trunk(x, prm, *, tm_qkv=1024, tk_a=512, sub_a=128, unroll_c=16, tm_tail=512,
           fchunk=256):
    b, t, _ = x.shape
    tq_a = min(A_MAX_QUERIES, A_SCORE_BYTES // (2 * 4 * t))
    depth = prm["w_qkv"].shape[0]
    rope = _rope_stack(t)
    bias, band, tq_c = prm["bias"], prm["band"], TQ_C
    for layer in range(depth):
        qkv = _qkv_proj(x, layer, prm["norm_mix"], prm["w_qkv"], prm["gains"], rope,
                        prm["head_ones"], tm_qkv)
        oa = _attn_a(qkv, tq_a, tk_a, sub_a)
        ob, oc = _attn_bc(qkv, bias, band, layer, tq_c, unroll_c)
        x = _tail(x, oa, ob, oc, layer, prm["out_gain"], prm["w_out"], prm["norm_ffn"],
                  prm["w_gu"], prm["w_down"], tm_tail, fchunk)
    return x


def kernel(x_prompt, x_sample, norm_mix, w_in, q_gain, k_gain, rpb, out_gain, w_out, norm_ffn,
           w_gate_up, w_down):
    prm = _prepare(norm_mix, w_in, q_gain, k_gain, rpb, out_gain, w_out, norm_ffn, w_gate_up,
                   w_down)
    return (_trunk(x_prompt, prm), _trunk(x_sample, prm))
```

```python
import functools

import jax
import jax.numpy as jnp
from jax import lax
from jax.experimental import pallas as pl
from jax.experimental.pallas import tpu as pltpu

F32 = jnp.float32
BF16 = jnp.bfloat16

D_MODEL = 1024
HEAD_DIM = 64
N_HEADS = 16
KV_HEADS = 8
QKV_WIDTH = 2048
FFN_DIM = 2816
GRID_W = 64
NA_ROWS = 8
NA_COLS = 16
STEP = 4
DILATIONS = (1, STEP, STEP * STEP)
HALF_WINDOW = 64
TQ_C = 2 * HALF_WINDOW
ROPE_THETA = 10000.0
RMS_EPS = 1e-6
NEG_INF = -1e30
LOG2E = 1.4426950408889634
SCALE = HEAD_DIM ** -0.5 * LOG2E

LANES = 128
PAIR = 2 * LANES
A_GROUPS, B_GROUPS, C_GROUPS = 2, 3, 3
Q_BLK = {"a": 0, "b": 2, "c": 5}
KV_BLK = {"a": 8, "b": 10, "c": 13}
N_BLKS = QKV_WIDTH // LANES

VMEM_LIMIT = 56 * 1024 * 1024
A_SCORE_BYTES = 32 * 1024 * 1024
FFN_CHUNK = 256
A_MAX_QUERIES = 1024


def _cparams(n_axes):
    return pltpu.CompilerParams(
        dimension_semantics=("arbitrary",) * n_axes, vmem_limit_bytes=VMEM_LIMIT)


def _resident(block_shape, index_map):
    return pl.BlockSpec(block_shape, index_map, pipeline_mode=pl.Buffered(1))


def _rms(x, gain):
    ms = jnp.mean(x * x, axis=-1, keepdims=True)
    return x * lax.rsqrt(ms + RMS_EPS) * gain


def _rotate(y, lane, half):
    up = pltpu.roll(y, LANES - half, 1)
    dn = pltpu.roll(y, half, 1)
    return jnp.where((lane % (2 * half)) < half, up, dn)


def _qkv_kernel(x_ref, nrm_ref, w_ref, gain_ref, rope_ref, ones_ref, out_ref, h_ref, *, chunk,
                sub):
    n_sub = x_ref.shape[0] // sub
    for s in range(n_sub):
        rows = slice(s * sub, (s + 1) * sub)
        h_ref[rows, :] = _rms(x_ref[rows, :], nrm_ref[...]).astype(BF16)
    lane = lax.broadcasted_iota(jnp.int32, (sub, LANES), 1)
    is_k = lane < HEAD_DIM
    for c in range(QKV_WIDTH // chunk):
        for s in range(n_sub):
            rows = slice(s * sub, (s + 1) * sub)
            wide = jnp.dot(h_ref[rows, :], w_ref[:, c * chunk:(c + 1) * chunk],
                           preferred_element_type=F32)
            for hb in range(chunk // LANES):
                if hb % 2 == 0:
                    p = wide[:, hb * LANES:hb * LANES + PAIR]
                    ss = jnp.dot((p * p).astype(BF16), ones_ref[...], preferred_element_type=F32)
                    inv = lax.rsqrt(ss * (1.0 / HEAD_DIM) + RMS_EPS)
                blk = c * (chunk // LANES) + hb
                cols = slice((hb % 2) * LANES, (hb % 2 + 1) * LANES)
                is_q = blk < KV_BLK["a"]
                scale = inv[:, cols] if is_q else jnp.where(is_k, inv[:, cols], 1.0)
                y = p[:, cols] * scale * gain_ref[:, blk * LANES:(blk + 1) * LANES]
                if blk < Q_BLK["b"]:
                    tab, half = 0, HEAD_DIM // 4
                elif blk < Q_BLK["c"]:
                    tab, half = None, None
                elif is_q:
                    tab, half = 4, HEAD_DIM // 2
                elif blk < KV_BLK["b"]:
                    tab, half = 2, HEAD_DIM // 4
                elif blk < KV_BLK["c"]:
                    tab, half = None, None
                else:
                    tab, half = 6, HEAD_DIM // 2
                if tab is not None:
                    y = (y * rope_ref[tab, rows, :]
                         + _rotate(y, lane, half) * rope_ref[tab + 1, rows, :])
                elif is_q:
                    y = y * SCALE
                out_ref[rows, blk * LANES:(blk + 1) * LANES] = y.astype(BF16)


def _qkv_proj(x, layer, nrm, w_in, gains, rope, ones, tm):
    b, t, d = x.shape
    return pl.pallas_call(
        functools.partial(_qkv_kernel, chunk=4 * LANES, sub=tm // 2),
        grid=(t // tm, b),
        in_specs=[
            pl.BlockSpec((None, tm, d), lambda i, bb: (bb, i, 0)),
            _resident((None, 1, d), lambda i, bb: (layer, 0, 0)),
            _resident((None, d, QKV_WIDTH), lambda i, bb: (layer, 0, 0)),
            _resident((None, 1, QKV_WIDTH), lambda i, bb: (layer, 0, 0)),
            pl.BlockSpec((8, tm, LANES), lambda i, bb: (0, i, 0)),
            _resident((PAIR, PAIR), lambda i, bb: (0, 0)),
        ],
        out_specs=pl.BlockSpec((None, tm, QKV_WIDTH), lambda i, bb: (bb, i, 0)),
        out_shape=jax.ShapeDtypeStruct((b, t, QKV_WIDTH), BF16),
        scratch_shapes=[pltpu.VMEM((tm, d), BF16)],
        compiler_params=_cparams(2),
        name="qkv_proj",
    )(x, nrm, w_in, gains, rope, ones)


def _stack_heads(q):
    q32 = q.astype(F32)
    lane = lax.broadcasted_iota(jnp.int32, q32.shape, 1)
    lo = lane < HEAD_DIM
    q0 = jnp.where(lo, q32, 0.0)
    q1 = jnp.where(lo, pltpu.roll(q32, HEAD_DIM, 1), 0.0)
    return jnp.concatenate([q0, q1], axis=0).astype(BF16)


def _scores(a, b):
    return lax.dot_general(a, b, (((1,), (1,)), ((), ())), preferred_element_type=F32)


def _ones_for_k(kv32):
    lane = lax.broadcasted_iota(jnp.int32, kv32.shape, 1)
    return jnp.where(lane < HEAD_DIM, 1.0, kv32).astype(BF16)


def _split_pv(o2):
    n = o2.shape[0] // 2
    lo = lax.broadcasted_iota(jnp.int32, (n, LANES), 1) < HEAD_DIM
    pv = jnp.where(lo, pltpu.roll(o2[:n], HEAD_DIM, 1), o2[n:])
    den = jnp.where(lo, o2[:n], pltpu.roll(o2[n:], HEAD_DIM, 1))
    return pv, den


def _attn_a_kernel(q_ref, kv_ref, o_ref, kv1_ref, s_ref, *, tk, sub):
    t = kv_ref.shape[0]
    n_sub = q_ref.shape[0] // sub
    w = 2 * sub

    @pl.when(pl.program_id(2) == 0)
    def _():
        kv1_ref[...] = _ones_for_k(kv_ref[...].astype(F32))

    tops = []
    for h in range(n_sub):
        q2 = _stack_heads(q_ref[h * sub:(h + 1) * sub, :])
        run_max = jnp.full((w, LANES), -jnp.inf, F32)
        for j in range(t // tk):
            sc = _scores(q2, kv_ref[j * tk:(j + 1) * tk, :])
            s_ref[h, :, j * tk:(j + 1) * tk] = sc
            for c in range(tk // LANES):
                run_max = jnp.maximum(run_max, sc[:, c * LANES:(c + 1) * LANES])
        tops.append(jnp.max(run_max, axis=-1, keepdims=True))

    for h in range(n_sub):
        acc = jnp.zeros((w, LANES), F32)
        for j in range(t // tk):
            p = jnp.exp2(s_ref[h, :, j * tk:(j + 1) * tk] - tops[h]).astype(BF16)
            acc = acc + jnp.dot(p, kv1_ref[j * tk:(j + 1) * tk, :], preferred_element_type=F32)
        pv, den = _split_pv(acc)
        o_ref[h * sub:(h + 1) * sub, :] = pv / den


def _attn_a(qkv, tq, tk, sub):
    b, t, _ = qkv.shape
    return pl.pallas_call(
        functools.partial(_attn_a_kernel, tk=tk, sub=sub),
        grid=(b, A_GROUPS, t // tq),
        in_specs=[
            pl.BlockSpec((None, tq, LANES), lambda bb, g, i: (bb, i, Q_BLK["a"] + g)),
            pl.BlockSpec((None, t, LANES), lambda bb, g, i: (bb, 0, KV_BLK["a"] + g)),
        ],
        out_specs=pl.BlockSpec((None, tq, LANES), lambda bb, g, i: (bb, i, g)),
        out_shape=jax.ShapeDtypeStruct((b, t, A_GROUPS * LANES), F32),
        scratch_shapes=[pltpu.VMEM((t, LANES), BF16), pltpu.VMEM((tq // sub, 2 * sub, t), F32)],
        compiler_params=_cparams(3),
        name="attn_global",
    )(qkv, qkv)


def _attn_b_kernel(q_ref, kv_ref, bias_ref, o_ref, kv1_ref, *, unroll):
    rows = q_ref.shape[0] // GRID_W
    n_keys = NA_ROWS * GRID_W
    kv1_ref[...] = _ones_for_k(kv_ref[...].astype(F32))

    def body(it, carry):
        offs, scores = [], []
        for u in range(unroll):
            r = it * unroll + u
            r0 = jnp.clip(r - NA_ROWS // 2, 0, rows - NA_ROWS)
            qoff = pl.multiple_of(r * GRID_W, GRID_W)
            koff = pl.multiple_of(r0 * GRID_W, GRID_W)
            q2 = _stack_heads(q_ref[pl.ds(qoff, GRID_W), :])
            dr = r0 - r + (NA_ROWS - 1)
            bias = jnp.concatenate(
                [jnp.concatenate([bias_ref[h, dr + 2 * j] for j in range(NA_ROWS // 2)], axis=1)
                 for h in range(2)], axis=0)
            scores.append(_scores(q2, kv_ref[pl.ds(koff, n_keys), :]) + bias)
            offs.append((qoff, koff))
        probs = [jnp.exp2(sc - jnp.max(sc, axis=-1, keepdims=True)).astype(BF16) for sc in scores]
        outs = [jnp.dot(p, kv1_ref[pl.ds(koff, n_keys), :], preferred_element_type=F32)
                for p, (_, koff) in zip(probs, offs)]
        for o2, (qoff, _) in zip(outs, offs):
            pv, den = _split_pv(o2)
            o_ref[pl.ds(qoff, GRID_W), :] = pv / den
        return carry

    lax.fori_loop(0, rows // unroll, body, 0)


def _attn_b(qkv, bias, layer, unroll):
    b, t, _ = qkv.shape
    return pl.pallas_call(
        functools.partial(_attn_b_kernel, unroll=unroll),
        grid=(B_GROUPS, b),
        in_specs=[
            pl.BlockSpec((None, t, LANES), lambda g, bb: (bb, 0, Q_BLK["b"] + g)),
            pl.BlockSpec((None, t, LANES), lambda g, bb: (bb, 0, KV_BLK["b"] + g)),
            pl.BlockSpec((None, None, 2, 2 * NA_ROWS - 2, GRID_W, 2 * GRID_W),
                         lambda g, bb: (layer, g, 0, 0, 0, 0)),
        ],
        out_specs=pl.BlockSpec((None, t, LANES), lambda g, bb: (bb, 0, g)),
        out_shape=jax.ShapeDtypeStruct((b, t, B_GROUPS * LANES), F32),
        scratch_shapes=[pltpu.VMEM((t, LANES), BF16)],
        compiler_params=_cparams(2),
        name="attn_neighbourhood",
    )(qkv, qkv, bias)


def _natural_residue(cls, level):
    res = 0
    for d in range(level):
        res = res + ((cls // (STEP ** (level - 1 - d))) % STEP) * (STEP ** d)
    return res


def _attn_c_kernel(q_ref, kv_ref, bias_ref, o_ref, qf_ref, kvf_ref, qg_ref, kvg_ref, qs_ref,
                   kvs_ref, kv1_ref, ma_ref, wa_ref, na_ref, mb_ref, wb_ref, nb_ref, *, tq,
                   unroll):
    t = q_ref.shape[0]
    lo = lax.broadcasted_iota(jnp.int32, (tq, LANES), 1) < HEAD_DIM
    qf_ref[...] = q_ref[...].astype(F32)
    kvf_ref[...] = kv_ref[...].astype(F32)
    f32_src = (qf_ref, kvf_ref)
    states = ((ma_ref, wa_ref, na_ref), (mb_ref, wb_ref, nb_ref))

    for level, dil in enumerate(DILATIONS):
        length = t // dil
        n_tiles = length // tq
        n_keys = min(length, tq + 2 * HALF_WINDOW)
        first, last = level == 0, level == len(DILATIONS) - 1
        prev_len = length * STEP
        st_in, st_out = states[(level + 1) % 2], states[level % 2]

        if first:
            q_src, kv_src = q_ref, kv_ref
            kv1_ref[...] = _ones_for_k(kvf_ref[...])
        else:
            q_src, kv_src = qs_ref, kvs_ref
            f32_dst = None if last else (qg_ref, kvg_ref)

            def gather_class(c, carry, length=length, prev_len=prev_len, f32_src=f32_src,
                             f32_dst=f32_dst):
                parent = c // STEP
                src = pl.ds(parent * prev_len + (c - parent * STEP), length, stride=STEP)
                rows = pl.ds(pl.multiple_of(c * length, LANES), length)
                qd = f32_src[0][src, :]
                kvd = f32_src[1][src, :]
                qs_ref[rows, :] = qd.astype(BF16)
                kvs_ref[rows, :] = kvd.astype(BF16)
                kv1_ref[rows, :] = _ones_for_k(kvd)
                if f32_dst is not None:
                    f32_dst[0][rows, :] = qd
                    f32_dst[1][rows, :] = kvd
                return carry

            lax.fori_loop(0, dil, gather_class, 0)
            f32_src = f32_dst

        def body(it, carry, level=level, dil=dil, length=length, n_tiles=n_tiles, n_keys=n_keys,
                 first=first, last=last, prev_len=prev_len, q_src=q_src, kv_src=kv_src,
                 st_in=st_in, st_out=st_out):
            where, scores = [], []
            for u in range(unroll):
                g = it * unroll + u
                c = g // n_tiles
                i = g - c * n_tiles
                a0 = i * tq
                start = jnp.clip(a0 - HALF_WINDOW, 0, length - n_keys)
                kind = jnp.where(i == 0, 0, jnp.where(i == n_tiles - 1, 2, 1))
                own = pl.ds(pl.multiple_of(g * tq, tq), tq)
                krow = pl.multiple_of(c * length + start, HALF_WINDOW)
                q2 = _stack_heads(q_src[own, :])
                scores.append(_scores(q2, kv_src[pl.ds(krow, n_keys), :])
                              + bias_ref[kind, :, :n_keys])
                parent = c // STEP
                prev = pl.ds(parent * prev_len + STEP * a0 + (c - parent * STEP), tq, stride=STEP)
                out = pl.ds(dil * a0 + _natural_residue(c, level), tq, stride=dil)
                where.append((krow, own, prev, out))
            tops = [jnp.max(sc, axis=-1, keepdims=True) for sc in scores]
            probs = [jnp.exp2(sc - m).astype(BF16) for sc, m in zip(scores, tops)]
            outs = [jnp.dot(p, kv1_ref[pl.ds(krow, n_keys), :], preferred_element_type=F32)
                    for p, (krow, _, _, _) in zip(probs, where)]
            for o2, m, (_, own, prev, out) in zip(outs, tops, where):
                pv, den = _split_pv(o2)
                top = jnp.where(lo, jnp.broadcast_to(m[:tq], (tq, LANES)),
                                jnp.broadcast_to(m[tq:], (tq, LANES)))
                if not first:
                    m_old = st_in[0][prev, :]
                    m_new = jnp.maximum(m_old, top)
                    e_old = jnp.exp2(m_old - m_new)
                    e_new = jnp.exp2(top - m_new)
                    pv = e_old * st_in[2][prev, :] + e_new * pv
                    den = e_old * st_in[1][prev, :] + e_new * den
                    top = m_new
                if last:
                    o_ref[out, :] = pv / den
                else:
                    st_out[0][own, :] = top
                    st_out[1][own, :] = den
                    st_out[2][own, :] = pv
            return carry

        lax.fori_loop(0, (t // tq) // unroll, body, 0)


def _band_bias(tq):
    row = jnp.arange(2 * tq)[:, None] % tq
    col = jnp.arange(tq + 2 * HALF_WINDOW)[None, :]
    delta = jnp.arange(3)[:, None, None] * HALF_WINDOW
    return jnp.where(jnp.abs(delta + row - col) <= HALF_WINDOW, 0.0, NEG_INF).astype(F32)


def _attn_c(qkv, band, tq, unroll):
    b, t, _ = qkv.shape
    f32 = pltpu.VMEM((t, LANES), F32)
    b16 = pltpu.VMEM((t, LANES), BF16)
    return pl.pallas_call(
        functools.partial(_attn_c_kernel, tq=tq, unroll=unroll),
        grid=(C_GROUPS, b),
        in_specs=[
            pl.BlockSpec((None, t, LANES), lambda g, bb: (bb, 0, Q_BLK["c"] + g)),
            pl.BlockSpec((None, t, LANES), lambda g, bb: (bb, 0, KV_BLK["c"] + g)),
            _resident(band.shape, lambda g, bb: (0, 0, 0)),
        ],
        out_specs=pl.BlockSpec((None, t, LANES), lambda g, bb: (bb, 0, g)),
        out_shape=jax.ShapeDtypeStruct((b, t, C_GROUPS * LANES), F32),
        scratch_shapes=[f32] * 4 + [b16] * 3 + [f32] * 6,
        compiler_params=_cparams(2),
        name="attn_dilated",
    )(qkv, qkv, band)


def _tail_kernel(x_ref, oa_ref, ob_ref, oc_ref, og_ref, wo_ref, nf_ref, wgu_ref, wd_ref,
                 out_ref, y_ref, act_ref, h_ref, *, fchunk, sub):
    a_w = A_GROUPS * LANES
    b_w = B_GROUPS * LANES
    subs = [slice(s * sub, (s + 1) * sub) for s in range(x_ref.shape[0] // sub)]
    for rows in subs:
        y_ref[rows, :a_w] = _rms(oa_ref[rows, :], og_ref[:, :a_w]).astype(BF16)
        y_ref[rows, a_w:a_w + b_w] = _rms(ob_ref[rows, :], og_ref[:, a_w:a_w + b_w]).astype(BF16)
        y_ref[rows, a_w + b_w:] = _rms(oc_ref[rows, :], og_ref[:, a_w + b_w:]).astype(BF16)
    for rows in subs:
        x1 = x_ref[rows, :] + jnp.dot(y_ref[rows, :], wo_ref[...], preferred_element_type=F32)
        out_ref[rows, :] = x1
        h_ref[rows, :] = _rms(x1, nf_ref[...]).astype(BF16)
    for c in range(FFN_DIM // fchunk):
        for rows in subs:
            h = h_ref[rows, :]
            gu = jnp.dot(h, wgu_ref[:, 2 * c * fchunk:2 * (c + 1) * fchunk],
                         preferred_element_type=F32)
            act_ref[rows, c * fchunk:(c + 1) * fchunk] = (
                jax.nn.silu(gu[:, :fchunk]) * gu[:, fchunk:]).astype(BF16)
    for rows in subs:
        out_ref[rows, :] = out_ref[rows, :] + jnp.dot(act_ref[rows, :], wd_ref[...],
                                                      preferred_element_type=F32)


def _tail(x, oa, ob, oc, layer, out_gain, w_out, norm_ffn, w_gu, w_down, tm, fchunk):
    b, t, d = x.shape
    n = b * t
    flat = lambda a: a.reshape(n, a.shape[-1])
    row = lambda w: pl.BlockSpec((tm, w), lambda i: (i, 0))
    out = pl.pallas_call(
        functools.partial(_tail_kernel, fchunk=fchunk, sub=tm // 2),
        grid=(n // tm,),
        in_specs=[
            row(d), row(A_GROUPS * LANES), row(B_GROUPS * LANES), row(C_GROUPS * LANES),
            _resident((None, 1, d), lambda i: (layer, 0, 0)),
            _resident((None, d, d), lambda i: (layer, 0, 0)),
            _resident((None, 1, d), lambda i: (layer, 0, 0)),
            _resident((None, d, 2 * FFN_DIM), lambda i: (layer, 0, 0)),
            _resident((None, FFN_DIM, d), lambda i: (layer, 0, 0)),
        ],
        out_specs=row(d),
        out_shape=jax.ShapeDtypeStruct((n, d), F32),
        scratch_shapes=[pltpu.VMEM((tm, d), BF16), pltpu.VMEM((tm, FFN_DIM), BF16),
                        pltpu.VMEM((tm, d), BF16)],
        compiler_params=_cparams(1),
        name="out_proj_ffn",
    )(flat(x), flat(oa), flat(ob), flat(oc),
      out_gain, w_out, norm_ffn, w_gu, w_down)
    return out.reshape(b, t, d)


def _rope_tables(pos, dim):
    inv_freq = ROPE_THETA ** (-jnp.arange(0, dim, 2, dtype=F32) / dim)
    ang = pos[:, None] * inv_freq[None, :]
    ang = jnp.concatenate([ang, ang], axis=-1)
    return jnp.cos(ang), jnp.sin(ang)


def _rope_stack(t):
    pos = jnp.arange(t, dtype=jnp.int32)
    half = HEAD_DIM // 2
    cos_r, sin_r = _rope_tables((pos // GRID_W).astype(F32), half)
    cos_c, sin_c = _rope_tables((pos % GRID_W).astype(F32), half)
    cos_1, sin_1 = _rope_tables(pos.astype(F32), HEAD_DIM)
    j = jnp.arange(HEAD_DIM)
    cos_a = jnp.concatenate([cos_r, cos_c], axis=-1)
    sin_a = jnp.concatenate([sin_r, sin_c], axis=-1) * jnp.where(j % half < half // 2, -1.0, 1.0)
    sin_1 = sin_1 * jnp.where(j < half, -1.0, 1.0)
    one, zero = jnp.ones_like(cos_a), jnp.zeros_like(cos_a)
    cat = lambda u, v: jnp.concatenate([u, v], axis=-1)
    return jnp.stack([
        cat(cos_a, cos_a) * SCALE, cat(sin_a, sin_a) * SCALE, cat(cos_a, one), cat(sin_a, zero),
        cat(cos_1, cos_1) * SCALE, cat(sin_1, sin_1) * SCALE, cat(cos_1, one), cat(sin_1, zero),
    ]).astype(F32)


def _neighbourhood_bias(rpb):
    depth = rpb.shape[0]
    qc = jnp.arange(GRID_W)[:, None]
    kc = jnp.arange(GRID_W)[None, :]
    c0 = jnp.clip(qc - NA_COLS // 2, 0, GRID_W - NA_COLS)
    valid = (kc >= c0) & (kc < c0 + NA_COLS)
    dc = jnp.clip(kc - qc + NA_COLS - 1, 0, 2 * NA_COLS - 2)
    band = jnp.where(valid, (rpb * LOG2E)[..., dc], NEG_INF)
    pairs = jnp.concatenate([band[:, :, :-1], band[:, :, 1:]], axis=-1)
    return pairs.reshape(depth, B_GROUPS, 2, 2 * NA_ROWS - 2, GRID_W, 2 * GRID_W).astype(F32)


def _pair_gate_up(w, chunk):
    depth, d, _ = w.shape
    w = w.reshape(depth, d, 2, FFN_DIM // chunk, chunk)
    return jnp.swapaxes(w, 2, 3).reshape(depth, d, 2 * FFN_DIM)


def _prepare(norm_mix, w_in, q_gain, k_gain, rpb, out_gain, w_out, norm_ffn, w_gate_up, w_down):
    depth = w_in.shape[0]
    q_w = N_HEADS * HEAD_DIM
    kv_w = KV_HEADS * HEAD_DIM
    w16 = w_in.astype(BF16)
    wq = w16[..., :q_w]
    wk = w16[..., q_w:q_w + kv_w].reshape(depth, D_MODEL, KV_HEADS, 1, HEAD_DIM)
    wv = w16[..., q_w + kv_w:].reshape(depth, D_MODEL, KV_HEADS, 1, HEAD_DIM)
    wkv = jnp.concatenate([wk, wv], axis=3).reshape(depth, D_MODEL, 2 * kv_w)
    w_qkv = jnp.concatenate([wq, wkv], axis=-1)

    heads_q = (4, 6, 6)
    heads_kv = (2, 3, 3)
    gq = jnp.concatenate([jnp.tile(q_gain[:, m], (1, heads_q[m])) for m in range(3)], axis=-1)
    ones = jnp.ones((depth, HEAD_DIM), F32)
    gkv = jnp.concatenate(
        [jnp.tile(jnp.concatenate([k_gain[:, m], ones], axis=-1), (1, heads_kv[m]))
         for m in range(3)], axis=-1)
    gains = jnp.concatenate([gq, gkv], axis=-1).reshape(depth, 1, QKV_WIDTH).astype(F32)

    seg = jnp.arange(PAIR) // HEAD_DIM
    head_ones = (seg[:, None] == seg[None, :]).astype(BF16)
    return dict(
        norm_mix=norm_mix.reshape(depth, 1, D_MODEL), w_qkv=w_qkv, gains=gains,
        head_ones=head_ones, out_gain=out_gain.reshape(depth, 1, D_MODEL),
        w_out=w_out.astype(BF16), norm_ffn=norm_ffn.reshape(depth, 1, D_MODEL),
        w_gu=_pair_gate_up(w_gate_up.astype(BF16), FFN_CHUNK), w_down=w_down.astype(BF16),
        bias=_neighbourhood_bias(rpb), band=_band_bias(TQ_C))


def _trunk(x, prm, *, tm_qkv=1024, tk_a=512, sub_a=128, unroll_b=32, unroll_c=16, tm_tail=512,
           fchunk=FFN_CHUNK):
    b, t, _ = x.shape
    tq_a = min(A_MAX_QUERIES, A_SCORE_BYTES // (2 * 4 * t))
    depth = prm["w_qkv"].shape[0]
    rope = _rope_stack(t)
    bias, band, tq_c = prm["bias"], prm["band"], TQ_C
    for layer in range(depth):
        qkv = _qkv_proj(x, layer, prm["norm_mix"], prm["w_qkv"], prm["gains"], rope,
                        prm["head_ones"], tm_qkv)
        oa = _attn_a(qkv, tq_a, tk_a, sub_a)
        ob = _attn_b(qkv, bias, layer, unroll_b)
        oc = _attn_c(qkv, band, tq_c, unroll_c)
        x = _tail(x, oa, ob, oc, layer, prm["out_gain"], prm["w_out"], prm["norm_ffn"],
                  prm["w_gu"], prm["w_down"], tm_tail, fchunk)
    return x


def kernel(x_prompt, x_sample, norm_mix, w_in, q_gain, k_gain, rpb, out_gain, w_out, norm_ffn,
           w_gate_up, w_down):
    prm = _prepare(norm_mix, w_in, q_gain, k_gain, rpb, out_gain, w_out, norm_ffn, w_gate_up,
                   w_down)
    return (_trunk(x_prompt, prm), _trunk(x_sample, prm))
```
